```python
import jax, jax.numpy as jnp
from jax import lax
import numpy as np

D_MODEL = 1024
BATCH = 16
SEQ = 2048
DEPTH = 2

CHUNK = 64
Q_BLOCK = 128
N_BRANCH = 3
ATTN_HEADS = 8
ATTN_HEAD_DIM = 64
ATTN_WIDTH = ATTN_HEADS * ATTN_HEAD_DIM
POOL_WINDOWS = (2, 4, 8, 16)
POOL_GROUPS = 4
POOL_WIDTH = 512
POOL_GROUP_DIM = POOL_WIDTH // POOL_GROUPS
CONV_WIDTH = 512
CONV_K = 3
FFN_HIDDEN = -(-(8 * D_MODEL) // (3 * 256)) * 256
RMS_EPS = 1e-6
NEG_INF = -1e30

IN_SPLITS = (ATTN_WIDTH, ATTN_WIDTH, ATTN_WIDTH, ATTN_HEADS, POOL_WIDTH,
             CONV_WIDTH, CONV_WIDTH, CONV_WIDTH, N_BRANCH * D_MODEL)
IN_COLS = sum(IN_SPLITS)
IN_SPLIT_POINTS = tuple(int(p) for p in np.cumsum(IN_SPLITS)[:-1])

kernel_name = "hybrid_fox_pool_shortconv_gated_block"


def rms_norm(x, g):
    xf = x.astype(jnp.float32)
    y = xf * lax.rsqrt(jnp.mean(xf * xf, axis=-1, keepdims=True) + RMS_EPS)
    return (y * g.astype(jnp.float32)).astype(x.dtype)


def forgetting_attention(q, k, v, f_logit, b_f):
    B, S, H, Dh = q.shape
    log_f = jax.nn.log_sigmoid((f_logit + b_f).astype(jnp.float32))
    F = jnp.cumsum(log_f, axis=1)
    F_k = F.transpose(0, 2, 1)
    nblk = S // Q_BLOCK
    q_blocks = q.reshape(B, nblk, Q_BLOCK, H, Dh).swapaxes(0, 1)
    F_q_blocks = F.reshape(B, nblk, Q_BLOCK, H).swapaxes(0, 1)
    q_pos_blocks = jnp.arange(S, dtype=jnp.int32).reshape(nblk, Q_BLOCK)
    k_pos = jnp.arange(S, dtype=jnp.int32)
    scale = Dh ** -0.5

    def one_block(args):
        q_i, F_q_i, pos_i = args
        logits = jnp.einsum('bqhd,bkhd->bhqk', q_i, k).astype(jnp.float32) * scale
        decay = F_q_i.transpose(0, 2, 1)[..., :, None] - F_k[..., None, :]
        mask = k_pos[None, :] <= pos_i[:, None]
        logits = jnp.where(mask, logits + decay, NEG_INF)
        p = jax.nn.softmax(logits, axis=-1)
        return jnp.einsum('bhqk,bkhd->bqhd', p.astype(v.dtype), v)

    out = lax.map(one_block, (q_blocks, F_q_blocks, q_pos_blocks))
    return out.swapaxes(0, 1).reshape(B, S, H * Dh)


def multiscale_pool(u, pool_w, pool_scale):
    B, S, C = u.shape
    uf = u.astype(jnp.float32)
    c0 = jnp.pad(jnp.cumsum(uf, axis=1), ((0, 0), (1, 0), (0, 0)))
    n_avail = jnp.arange(1, S + 1, dtype=jnp.float32)[:, None]
    outs = []
    for g, w in enumerate(POOL_WINDOWS):
        sl = slice(g * POOL_GROUP_DIM, (g + 1) * POOL_GROUP_DIM)
        cg = c0[..., sl]
        lag = jnp.pad(cg, ((0, 0), (w, 0), (0, 0)))[:, :S + 1]
        mean = (cg - lag)[:, 1:] / jnp.minimum(n_avail, float(w))
        outs.append(mean - uf[..., sl])
    d = jnp.stack(outs, axis=2).astype(u.dtype)
    y = jnp.einsum('bsgc,gcd->bsgd', d, pool_w).reshape(B, S, C)
    return y * pool_scale


def short_gated_conv(v, gate_b, gate_c, conv_w):
    C = v.shape[-1]
    z = gate_c * v
    y = lax.conv_general_dilated(z, conv_w[:, None, :], window_strides=(1,),
                                 padding=[(CONV_K - 1, 0)],
                                 dimension_numbers=('NWC', 'WIO', 'NWC'),
                                 feature_group_count=C)
    return gate_b * y


def setup_inputs(seed: int = 0) -> dict:
    key = jax.random.key(seed)
    ks = jax.random.split(key, 18)
    f32 = jnp.float32

    def nrm(k, shape, fan_in):
        return jax.random.normal(k, shape, f32) * (fan_in ** -0.5)

    return {
        "x": jax.random.normal(ks[0], (BATCH, SEQ, D_MODEL), f32),
        "attn_norm": 1.0 + 0.05 * jax.random.normal(ks[1], (DEPTH, D_MODEL), f32),
        "w_in": nrm(ks[2], (DEPTH, D_MODEL, IN_COLS), D_MODEL),
        "b_forget": jax.random.uniform(ks[3], (DEPTH, ATTN_HEADS), f32, 1.0, 4.0),
        "b_gate": 0.1 * jax.random.normal(ks[4], (DEPTH, N_BRANCH * D_MODEL), f32),
        "w_proj_attn": nrm(ks[5], (DEPTH, ATTN_WIDTH, D_MODEL), ATTN_WIDTH),
        "pool_w": nrm(ks[6], (DEPTH, POOL_GROUPS, POOL_GROUP_DIM, POOL_GROUP_DIM), POOL_GROUP_DIM),
        "pool_scale": 1.0 + 0.1 * jax.random.normal(ks[7], (DEPTH, POOL_WIDTH), f32),
        "w_proj_pool": nrm(ks[8], (DEPTH, POOL_WIDTH, D_MODEL), POOL_WIDTH),
        "conv_w": nrm(ks[9], (DEPTH, CONV_K, CONV_WIDTH), CONV_K),
        "w_proj_conv": nrm(ks[10], (DEPTH, CONV_WIDTH, D_MODEL), CONV_WIDTH),
        "w_out": nrm(ks[11], (DEPTH, D_MODEL, D_MODEL), D_MODEL),
        "ffn_norm": 1.0 + 0.05 * jax.random.normal(ks[12], (DEPTH, D_MODEL), f32),
        "w_gate_up": nrm(ks[13], (DEPTH, D_MODEL, 2 * FFN_HIDDEN), D_MODEL),
        "w_down": nrm(ks[14], (DEPTH, FFN_HIDDEN, D_MODEL), FFN_HIDDEN),
        "final_norm": 1.0 + 0.05 * jax.random.normal(ks[15], (D_MODEL,), f32),
    }


def reference(x, attn_norm, w_in, b_forget, b_gate, w_proj_attn, pool_w, pool_scale,
              w_proj_pool, conv_w, w_proj_conv, w_out, ffn_norm, w_gate_up, w_down,
              final_norm):
    B, S, D = x.shape
    for l in range(DEPTH):
        h = rms_norm(x, attn_norm[l])
        proj = h @ w_in[l]
        q, k, v, f_logit, u, cv, cb, cc, g = jnp.split(proj, IN_SPLIT_POINTS, axis=-1)
        heads = (B, S, ATTN_HEADS, ATTN_HEAD_DIM)
        y_a = forgetting_attention(q.reshape(heads), k.reshape(heads), v.reshape(heads),
                                   f_logit, b_forget[l]) @ w_proj_attn[l]
        y_b = multiscale_pool(u, pool_w[l], pool_scale[l]) @ w_proj_pool[l]
        y_c = short_gated_conv(cv, cb, cc, conv_w[l]) @ w_proj_conv[l]
        gates = jax.nn.sigmoid(g + b_gate[l]).reshape(B, S, N_BRANCH, D)
        mixed = gates[..., 0, :] * y_a + gates[..., 1, :] * y_b + gates[..., 2, :] * y_c
        x = x + mixed @ w_out[l]
        h = rms_norm(x, ffn_norm[l])
        a, b = jnp.split(h @ w_gate_up[l], 2, axis=-1)
        x = x + (jax.nn.silu(a) * b) @ w_down[l]
    return rms_norm(x, final_norm)
```

```python
import functools

import jax
import jax.numpy as jnp
from jax import lax
from jax.experimental import pallas as pl
from jax.experimental.pallas import tpu as pltpu

D_MODEL = 1024
HEADS = 8
HEAD_DIM = 64
BRANCH_W = 512
POOL_WINDOWS = (2, 4, 8, 16)
POOL_GROUP_DIM = 128
CONV_K = 3
FFN_HIDDEN = 2816
FFN_CHUNKS = ((0, 1536), (1536, 2816))
RMS_EPS = 1e-6
NEG_INF = -1e30
LANES = 128
HALO = 16
HEADS_PER_GROUP = 4
GROUP_W = HEADS_PER_GROUP * HEAD_DIM

BF16 = jnp.bfloat16
F32 = jnp.float32

TM_PROJ = 512
TM_MIX = 512
TM_FFN = 512
TQ = 256
TK = 256
CUMSUM_CHUNK = 256
VMEM_LIMIT = 56 * 1024 * 1024


def _rms(x, w):
    return x * lax.rsqrt(jnp.mean(x * x, axis=-1, keepdims=True) + RMS_EPS) * w


def _const_spec(shape):
    nd = len(shape)
    return pl.BlockSpec(shape, lambda *_: (0,) * nd, pipeline_mode=pl.Buffered(1))


def _params(sem):
    return pltpu.CompilerParams(dimension_semantics=sem, vmem_limit_bytes=VMEM_LIMIT)


def _inproj_kernel(x_ref, nw_ref, w_ref, wf_ref, bf_ref,
                   q_ref, k_ref, v_ref, u_ref, cv_ref, cb_ref, cc_ref, g_ref, lf_ref):
    h = _rms(x_ref[...], nw_ref[...]).astype(BF16)
    for n, o_ref in enumerate((q_ref, k_ref, v_ref, u_ref, cv_ref, cb_ref, cc_ref)):
        lo = n * BRANCH_W
        o_ref[...] = jnp.dot(h, w_ref[:, lo:lo + BRANCH_W],
                             preferred_element_type=F32).astype(BF16)
    g0 = 7 * BRANCH_W
    for n in range(3 * D_MODEL // BRANCH_W):
        lo = n * BRANCH_W
        g_ref[:, lo:lo + BRANCH_W] = jnp.dot(h, w_ref[:, g0 + lo:g0 + lo + BRANCH_W],
                                             preferred_element_type=F32).astype(BF16)
    f = jnp.dot(h, wf_ref[...], preferred_element_type=F32) + bf_ref[...]
    lf_ref[...] = jnp.minimum(f, 0.0) - jnp.log1p(jnp.exp(-jnp.abs(f)))


def _inproj(x2, nw, w_main, w_f, b_f):
    n = x2.shape[0]
    tm = TM_PROJ
    row = lambda c: pl.BlockSpec((tm, c), lambda i: (i, 0))
    outs = [jax.ShapeDtypeStruct((n, BRANCH_W), BF16)] * 7
    outs += [jax.ShapeDtypeStruct((n, 3 * D_MODEL), BF16), jax.ShapeDtypeStruct((n, LANES), F32)]
    return pl.pallas_call(
        _inproj_kernel,
        grid=(n // tm,),
        in_specs=[row(D_MODEL), _const_spec(nw.shape), _const_spec(w_main.shape),
                  _const_spec(w_f.shape), _const_spec(b_f.shape)],
        out_specs=[row(BRANCH_W)] * 7 + [row(3 * D_MODEL), row(LANES)],
        out_shape=outs,
        compiler_params=_params(("parallel",)),
        name="inproj",
    )(x2, nw, w_main, w_f, b_f)


def _cumsum_kernel(lf_ref, o_ref):
    c = CUMSUM_CHUNK
    r = lax.broadcasted_iota(jnp.int32, (c, c), 0)
    s = lax.broadcasted_iota(jnp.int32, (c, c), 1)
    tri = (s <= r).astype(BF16)
    carry = jnp.zeros((1, LANES), F32)
    for n in range(lf_ref.shape[0] // c):
        x = lf_ref[n * c:(n + 1) * c, :]
        hi = x.astype(BF16)
        r1 = x - hi.astype(F32)
        mid = r1.astype(BF16)
        lo = (r1 - mid.astype(F32)).astype(BF16)
        parts = jnp.dot(tri, jnp.concatenate([hi, mid, lo], axis=1), preferred_element_type=F32)
        out = parts[:, :LANES] + parts[:, LANES:2 * LANES] + parts[:, 2 * LANES:] + carry
        o_ref[n * c:(n + 1) * c, :] = out
        carry = out[c - 1:c, :]


def _cumsum(lf, batch, seq):
    return pl.pallas_call(
        _cumsum_kernel,
        grid=(batch,),
        in_specs=[pl.BlockSpec((seq, LANES), lambda b: (b, 0))],
        out_specs=pl.BlockSpec((seq, LANES), lambda b: (b, 0)),
        out_shape=jax.ShapeDtypeStruct(lf.shape, F32),
        compiler_params=_params(("parallel",)),
        name="forget_cumsum",
    )(lf)


def _attn_kernel(q_ref, k_ref, v_ref, fq_ref, fk_ref, o_ref, *, tq, tk):
    qi = pl.program_id(1)
    lane_head = lax.broadcasted_iota(jnp.int32, (1, GROUP_W), 1) // HEAD_DIM
    row = lax.broadcasted_iota(jnp.int32, (tq, tk), 0)
    col = lax.broadcasted_iota(jnp.int32, (tq, tk), 1)
    causal = col <= row
    fq_all = fq_ref[...]

    def per_lane(cols):
        out = cols[HEADS_PER_GROUP - 1]
        for h in range(HEADS_PER_GROUP - 2, -1, -1):
            out = jnp.where(lane_head == h, cols[h], out)
        return out

    for g in range(HEADS // HEADS_PER_GROUP):
        gs = slice(g * GROUP_W, (g + 1) * GROUP_W)
        q4 = q_ref[:, gs]
        qm = [jnp.where(lane_head == h, q4, jnp.zeros_like(q4)) for h in range(HEADS_PER_GROUP)]
        fqs = [fq_all[:, g * HEADS_PER_GROUP + h:g * HEADS_PER_GROUP + h + 1]
               for h in range(HEADS_PER_GROUP)]

        def step(j, carry, masked):
            ms, ls, acc = carry
            ks = pl.multiple_of(j * tk, tk)
            k4 = k_ref[pl.ds(ks, tk), gs]
            v4 = v_ref[pl.ds(ks, tk), gs]
            fk = fk_ref[0, j]
            ps, alphas, new_ms, new_ls = [], [], [], []
            for h in range(HEADS_PER_GROUP):
                hh = g * HEADS_PER_GROUP + h
                s = lax.dot_general(qm[h], k4, (((1,), (1,)), ((), ())),
                                    preferred_element_type=F32)
                fk_h = fk[hh:hh + 1, :]
                f_ref = fk_h[:, 0:1]
                t = s - (fk_h - f_ref)
                if masked:
                    t = jnp.where(causal, t, NEG_INF)
                dq = fqs[h] - f_ref
                m_new = jnp.maximum(ms[h], jnp.max(t, axis=1, keepdims=True) + dq)
                p = jnp.exp(t - (m_new - dq))
                alpha = jnp.exp(ms[h] - m_new)
                new_ls.append(alpha * ls[h] + jnp.sum(p, axis=1, keepdims=True))
                new_ms.append(m_new)
                alphas.append(alpha)
                ps.append(p.astype(BF16))
            p_cat = jnp.concatenate(ps, axis=1)
            v_bd = jnp.concatenate(
                [jnp.where(lane_head == h, v4, jnp.zeros_like(v4)) for h in range(HEADS_PER_GROUP)],
                axis=0)
            pv = jnp.dot(p_cat, v_bd, preferred_element_type=F32)
            acc = acc * per_lane(alphas) + pv
            return new_ms, new_ls, acc

        init = ([jnp.full((tq, 1), NEG_INF, F32)] * HEADS_PER_GROUP,
                [jnp.zeros((tq, 1), F32)] * HEADS_PER_GROUP,
                jnp.zeros((tq, GROUP_W), F32))
        carry = lax.fori_loop(0, qi, functools.partial(step, masked=False), init)
        _, ls, acc = step(qi, carry, True)
        o_ref[:, gs] = (acc / per_lane(ls)).astype(BF16)


def _attention(q, k, v, f_cum, f_k, batch, seq):
    tq, tk = TQ, TK
    assert tq == tk
    nq = seq // tq
    w = HEADS * HEAD_DIM
    return pl.pallas_call(
        functools.partial(_attn_kernel, tq=tq, tk=tk),
        grid=(batch, nq),
        in_specs=[pl.BlockSpec((tq, w), lambda b, i: (b * nq + i, 0)),
                  pl.BlockSpec((seq, w), lambda b, i: (b, 0)),
                  pl.BlockSpec((seq, w), lambda b, i: (b, 0)),
                  pl.BlockSpec((tq, LANES), lambda b, i: (b * nq + i, 0)),
                  pl.BlockSpec((1, seq // tk, HEADS, tk), lambda b, i: (b, 0, 0, 0))],
        out_specs=pl.BlockSpec((tq, w), lambda b, i: (b * nq + i, 0)),
        out_shape=jax.ShapeDtypeStruct(q.shape, BF16),
        compiler_params=_params(("parallel", "arbitrary")),
        name="fox_attention",
    )(q, k, v, f_cum, f_k)


def _mix_kernel(a_ref, u_ref, up_ref, cv_ref, cvp_ref, cc_ref, ccp_ref, cb_ref, g_ref, x_ref,
                bg_ref, wpa_ref, pw_ref, ps_ref, wpp_ref, cw_ref, wpc_ref, wo_ref,
                o_ref, ue_ref, ze_ref, *, tm, tiles_per_seq):
    i = pl.program_id(0) % tiles_per_seq
    has_prev = i != 0

    u = u_ref[...].astype(F32)
    ue_ref[0:HALO, :] = jnp.where(has_prev, up_ref[...].astype(F32), 0.0)
    ue_ref[HALO:, :] = u
    z_prev = cvp_ref[...].astype(F32) * ccp_ref[...].astype(F32)
    ze_ref[0:HALO, :] = jnp.where(has_prev, z_prev, 0.0)
    ze_ref[HALO:, :] = cv_ref[...].astype(F32) * cc_ref[...].astype(F32)

    y_a = jnp.dot(a_ref[...], wpa_ref[...], preferred_element_type=F32)

    pos = i * tm + lax.broadcasted_iota(jnp.int32, (tm, 1), 0)
    ys = []
    for gi, w in enumerate(POOL_WINDOWS):
        cs = slice(gi * POOL_GROUP_DIM, (gi + 1) * POOL_GROUP_DIM)
        acc = u[:, cs]
        for lag in range(1, w):
            acc = acc + ue_ref[HALO - lag:HALO - lag + tm, cs]
        n_avail = jnp.minimum(pos + 1, w).astype(F32)
        d = acc / n_avail - u[:, cs]
        y = jnp.dot(d.astype(BF16), pw_ref[gi], preferred_element_type=F32)
        ys.append((y * ps_ref[:, cs]).astype(BF16))
    y_b = jnp.dot(jnp.concatenate(ys, axis=1), wpp_ref[...], preferred_element_type=F32)

    conv = cw_ref[CONV_K - 1:CONV_K, :] * ze_ref[HALO:, :]
    for lag in range(1, CONV_K):
        conv = conv + cw_ref[CONV_K - 1 - lag:CONV_K - lag, :] * ze_ref[HALO - lag:HALO - lag + tm, :]
    y_c = jnp.dot((cb_ref[...].astype(F32) * conv).astype(BF16), wpc_ref[...],
                  preferred_element_type=F32)

    mixed = None
    for n, y in enumerate((y_a, y_b, y_c)):
        cs = slice(n * D_MODEL, (n + 1) * D_MODEL)
        gate = jax.nn.sigmoid(g_ref[:, cs].astype(F32) + bg_ref[:, cs])
        mixed = gate * y if mixed is None else mixed + gate * y
    o_ref[...] = x_ref[...] + jnp.dot(mixed.astype(BF16), wo_ref[...], preferred_element_type=F32)


def _mix(a, u, cv, cc, cb, g, x2, bg, wpa, pw, ps, wpp, cw, wpc, wo, seq):
    n = x2.shape[0]
    tm = TM_MIX
    tiles_per_seq = seq // tm
    row = lambda c: pl.BlockSpec((tm, c), lambda i: (i, 0))
    prev = pl.BlockSpec((HALO, BRANCH_W), lambda i: (jnp.maximum(i * (tm // HALO) - 1, 0), 0))
    return pl.pallas_call(
        functools.partial(_mix_kernel, tm=tm, tiles_per_seq=tiles_per_seq),
        grid=(n // tm,),
        in_specs=[row(BRANCH_W), row(BRANCH_W), prev, row(BRANCH_W), prev, row(BRANCH_W), prev,
                  row(BRANCH_W), row(3 * D_MODEL), row(D_MODEL)]
                 + [_const_spec(t.shape) for t in (bg, wpa, pw, ps, wpp, cw, wpc, wo)],
        out_specs=row(D_MODEL),
        out_shape=jax.ShapeDtypeStruct(x2.shape, F32),
        scratch_shapes=[pltpu.VMEM((tm + HALO, BRANCH_W), F32),
                        pltpu.VMEM((tm + HALO, BRANCH_W), F32)],
        compiler_params=_params(("parallel",)),
        name="branch_mix",
    )(a, u, u, cv, cv, cc, cc, cb, g, x2, bg, wpa, pw, ps, wpp, cw, wpc, wo)


def _ffn_kernel(x_ref, nw_ref, wg_ref, wu_ref, wd_ref, fn_ref, o_ref, *, final):
    x = x_ref[...]
    h = _rms(x, nw_ref[...]).astype(BF16)
    acc = x
    for lo, hi in FFN_CHUNKS:
        a = jnp.dot(h, wg_ref[:, lo:hi], preferred_element_type=F32)
        b = jnp.dot(h, wu_ref[:, lo:hi], preferred_element_type=F32)
        act = (a * jax.nn.sigmoid(a) * b).astype(BF16)
        acc = acc + jnp.dot(act, wd_ref[lo:hi, :], preferred_element_type=F32)
    if final:
        acc = _rms(acc, fn_ref[...])
    o_ref[...] = acc


def _ffn(x2, nw, wg, wu, wd, fn, final):
    n = x2.shape[0]
    tm = TM_FFN
    row = pl.BlockSpec((tm, D_MODEL), lambda i: (i, 0))
    return pl.pallas_call(
        functools.partial(_ffn_kernel, final=final),
        grid=(n // tm,),
        in_specs=[row] + [_const_spec(t.shape) for t in (nw, wg, wu, wd, fn)],
        out_specs=row,
        out_shape=jax.ShapeDtypeStruct(x2.shape, F32),
        compiler_params=_params(("parallel",)),
        name="swiglu_ffn",
    )(x2, nw, wg, wu, wd, fn)


def kernel(x, attn_norm, w_in, b_forget, b_gate, w_proj_attn, pool_w, pool_scale, w_proj_pool,
           conv_w, w_proj_conv, w_out, ffn_norm, w_gate_up, w_down, final_norm):
    batch, seq, d = x.shape
    depth = w_in.shape[0]
    x2 = x.reshape(batch * seq, d)
    qkv_w = 3 * BRANCH_W
    fn = final_norm.reshape(1, d)
    for l in range(depth):
        wl = w_in[l]
        w_main = jnp.concatenate([wl[:, :BRANCH_W] * (HEAD_DIM ** -0.5), wl[:, BRANCH_W:qkv_w],
                                  wl[:, qkv_w + HEADS:]], axis=1).astype(BF16)
        w_f = jnp.pad(wl[:, qkv_w:qkv_w + HEADS], ((0, 0), (0, LANES - HEADS))).astype(BF16)
        b_f = jnp.pad(b_forget[l], (0, LANES - HEADS)).reshape(1, LANES)
        q, k, v, u, cv, cb, cc, g, lf = _inproj(x2, attn_norm[l].reshape(1, d), w_main, w_f, b_f)

        f_cum = _cumsum(lf, batch, seq)
        f_k = f_cum[:, :HEADS].reshape(batch, seq // TK, TK, HEADS).transpose(0, 1, 3, 2)
        a = _attention(q, k, v, f_cum, f_k, batch, seq)

        x2 = _mix(a, u, cv, cc, cb, g, x2, b_gate[l].reshape(1, -1),
                  w_proj_attn[l].astype(BF16), pool_w[l].astype(BF16),
                  pool_scale[l].reshape(1, -1), w_proj_pool[l].astype(BF16),
                  conv_w[l], w_proj_conv[l].astype(BF16), w_out[l].astype(BF16), seq)

        x2 = _ffn(x2, ffn_norm[l].reshape(1, d), w_gate_up[l][:, :FFN_HIDDEN].astype(BF16),
                  w_gate_up[l][:, FFN_HIDDEN:].astype(BF16), w_down[l].astype(BF16), fn,
                  final=(l == depth - 1))
    return x2.reshape(batch, seq, d)
```

```python
import functools
import math

import jax
import jax.numpy as jnp
from jax import lax
from jax.experimental import pallas as pl
from jax.experimental.pallas import tpu as pltpu

D_MODEL = 1024
HEADS = 8
HEAD_DIM = 64
BRANCH_W = 512
POOL_WINDOWS = (2, 4, 8, 16)
POOL_GROUP_DIM = 128
CONV_K = 3
FFN_HIDDEN = 2816
FFN_CHUNKS = ((0, 1536), (1536, 2816))
RMS_EPS = 1e-6
NEG_INF = -1e30
LOG2E = math.log2(math.e)
LANES = 128
SUBLANES = 8
HALO = 16
HEADS_PER_SLAB = LANES // HEAD_DIM
F_TERMS = 3

BF16 = jnp.bfloat16
F32 = jnp.float32

TM_PROJ = 512
TM_MIX = 512
TM_FFN = 512
TQ = 256
TK = 256
CUMSUM_CHUNK = 256
QK_LOOKAHEAD = 4
VMEM_LIMIT = 56 * 1024 * 1024


def _rms(x, w):
    return x * lax.rsqrt(jnp.mean(x * x, axis=-1, keepdims=True) + RMS_EPS) * w


def _const_spec(shape):
    nd = len(shape)
    return pl.BlockSpec(shape, lambda *_: (0,) * nd, pipeline_mode=pl.Buffered(1))


def _params(sem):
    return pltpu.CompilerParams(dimension_semantics=sem, vmem_limit_bytes=VMEM_LIMIT)


def _inproj_kernel(x_ref, nw_ref, w_ref, wf_ref, bf_ref,
                   qt_ref, k_ref, vt_ref, u_ref, cv_ref, cb_ref, cc_ref, g_ref, lf_ref):
    h = _rms(x_ref[...], nw_ref[...]).astype(BF16)

    def proj(n):
        return jnp.dot(h, w_ref[:, n * BRANCH_W:(n + 1) * BRANCH_W], preferred_element_type=F32)

    qt = proj(0).T.astype(BF16)
    for c in range(qt_ref.shape[1]):
        qt_ref[0, c] = qt[:, c * TQ:(c + 1) * TQ]
    k_ref[...] = proj(1).astype(BF16)
    vt = proj(2).T.astype(BF16)
    for c in range(vt_ref.shape[1]):
        vt_ref[0, c] = vt[:, c * TK:(c + 1) * TK]
    for n, o_ref in enumerate((u_ref, cv_ref, cb_ref, cc_ref), start=3):
        o_ref[...] = proj(n).astype(BF16)
    for n in range(3 * D_MODEL // BRANCH_W):
        g_ref[:, n * BRANCH_W:(n + 1) * BRANCH_W] = proj(7 + n).astype(BF16)
    f = jnp.dot(h, wf_ref[...], preferred_element_type=F32) + bf_ref[...]
    lf_ref[...] = jnp.minimum(f, 0.0) - jnp.log1p(jnp.exp(-jnp.abs(f)))


def _inproj(x2, nw, w_main, w_f, b_f, batch, seq):
    n = x2.shape[0]
    tm = TM_PROJ
    tiles_per_seq = seq // tm
    row = lambda c: pl.BlockSpec((tm, c), lambda i: (i, 0))
    assert TQ == TK
    vt_spec = pl.BlockSpec((1, tm // TK, BRANCH_W, TK),
                           lambda i: (i // tiles_per_seq, i % tiles_per_seq, 0, 0))
    act = jax.ShapeDtypeStruct((n, BRANCH_W), BF16)
    act_t = jax.ShapeDtypeStruct((batch, seq // TK, BRANCH_W, TK), BF16)
    outs = [act_t, act, act_t, act, act, act, act,
            jax.ShapeDtypeStruct((n, 3 * D_MODEL), BF16), jax.ShapeDtypeStruct((n, LANES), F32)]
    return pl.pallas_call(
        _inproj_kernel,
        grid=(n // tm,),
        in_specs=[row(D_MODEL), _const_spec(nw.shape), _const_spec(w_main.shape),
                  _const_spec(w_f.shape), _const_spec(b_f.shape)],
        out_specs=[vt_spec, row(BRANCH_W), vt_spec] + [row(BRANCH_W)] * 4
                  + [row(3 * D_MODEL), row(LANES)],
        out_shape=outs,
        compiler_params=_params(("parallel",)),
        name="inproj",
    )(x2, nw, w_main, w_f, b_f)


def _cumsum_kernel(lf_ref, fqt_ref, ka_ref):
    c = CUMSUM_CHUNK
    r = lax.broadcasted_iota(jnp.int32, (c, c), 0)
    s = lax.broadcasted_iota(jnp.int32, (c, c), 1)
    tri = (s <= r).astype(BF16)

    src = lax.broadcasted_iota(jnp.int32, (F_TERMS * LANES, BRANCH_W), 0)
    dst = lax.broadcasted_iota(jnp.int32, (F_TERMS * LANES, BRANCH_W), 1)
    head, term = src % LANES, src // LANES
    want = (head // HEADS_PER_SLAB) * LANES + F_TERMS * (head % HEADS_PER_SLAB) + term
    place = ((head < HEADS) & (dst == want)).astype(BF16)

    def split(x):
        hi = x.astype(BF16)
        r1 = x - hi.astype(F32)
        mid = r1.astype(BF16)
        lo = (r1 - mid.astype(F32)).astype(BF16)
        return jnp.concatenate([hi, mid, lo], axis=1)

    carry = jnp.zeros((1, LANES), F32)
    for n in range(lf_ref.shape[0] // c):
        rows = slice(n * c, (n + 1) * c)
        parts = jnp.dot(tri, split(lf_ref[rows, :]), preferred_element_type=F32)
        cum = parts[:, :LANES] + parts[:, LANES:2 * LANES] + parts[:, 2 * LANES:] + carry
        carry = cum[c - 1:c, :]
        f2 = cum * LOG2E
        fqt_ref[0, :, rows] = f2.T[:HEADS, :]
        ka_ref[rows, :] = jnp.dot(split(-f2), place, preferred_element_type=F32).astype(BF16)


def _cumsum(lf, batch, seq):
    return pl.pallas_call(
        _cumsum_kernel,
        grid=(batch,),
        in_specs=[pl.BlockSpec((seq, LANES), lambda b: (b, 0))],
        out_specs=[pl.BlockSpec((1, HEADS, seq), lambda b: (b, 0, 0)),
                   pl.BlockSpec((seq, BRANCH_W), lambda b: (b, 0))],
        out_shape=[jax.ShapeDtypeStruct((batch, HEADS, seq), F32),
                   jax.ShapeDtypeStruct((batch * seq, BRANCH_W), BF16)],
        compiler_params=_params(("parallel",)),
        name="forget_cumsum",
    )(lf)


def _sublane_allreduce(x, op):
    shift = SUBLANES // 2
    while shift:
        x = op(x, pltpu.roll(x, shift, axis=0))
        shift //= 2
    return x


def _attn_kernel(qt_ref, k_ref, ka_ref, vt_ref, fq_ref, o_ref,
                 qa_ref, st_ref, m_ref, l_ref, acc_ref, *, tq, tk):
    qi = pl.program_id(1)
    groups = tk // SUBLANES
    chan = lax.broadcasted_iota(jnp.int32, (LANES, 1), 0)
    for h in range(HEADS):
        slab, e = divmod(h, HEADS_PER_SLAB)
        q2 = qt_ref[0, 0, slab * LANES:(slab + 1) * LANES, :]
        qm = jnp.where(chan // HEAD_DIM == e, q2, jnp.zeros_like(q2))
        ones = ((chan >= F_TERMS * e) & (chan < F_TERMS * (e + 1))).astype(BF16)
        qa_ref[h] = jnp.concatenate([qm, jnp.broadcast_to(ones, (LANES, tq))], axis=0)
    m_ref[...] = jnp.full(m_ref.shape, NEG_INF, F32)
    l_ref[...] = jnp.zeros(l_ref.shape, F32)
    acc_ref[...] = jnp.zeros(acc_ref.shape, F32)

    key_pos = lax.broadcasted_iota(jnp.int32, (tk, tq), 0)
    query_pos = lax.broadcasted_iota(jnp.int32, (tk, tq), 1)
    causal = key_pos <= query_pos

    def logits_t(j, h):
        rows = pl.ds(pl.multiple_of(j * tk, tk), tk)
        cols = slice(h // HEADS_PER_SLAB * LANES, (h // HEADS_PER_SLAB + 1) * LANES)
        lhs = jnp.concatenate([k_ref[rows, cols], ka_ref[rows, cols]], axis=1)
        return jnp.dot(lhs, qa_ref[h], preferred_element_type=F32)

    def step(j, last):
        pending = []
        for h in range(HEADS):
            ahead = h + QK_LOOKAHEAD
            if ahead < HEADS:
                pending.append(logits_t(j, ahead))
            elif not last:
                st_ref[ahead - HEADS] = logits_t(j + 1, ahead - HEADS)
            st = st_ref[h] if h < QK_LOOKAHEAD else pending.pop(0)
            if last:
                st = jnp.where(causal, st, NEG_INF)
            st3 = st.reshape(groups, SUBLANES, tq)
            stat = slice(h * SUBLANES, (h + 1) * SUBLANES)
            fq = jnp.broadcast_to(fq_ref[0, h:h + 1, :], (SUBLANES, tq))
            m_old = m_ref[stat, :]
            m_new = jnp.maximum(m_old, _sublane_allreduce(jnp.max(st3, axis=0), jnp.maximum) + fq)
            p3 = jnp.exp2(st3 - (m_new - fq)[None])
            alpha = jnp.exp2(m_old - m_new)
            m_ref[stat, :] = m_new
            l_ref[stat, :] = alpha * l_ref[stat, :] + jnp.sum(p3, axis=0)
            ch = slice(h * HEAD_DIM, (h + 1) * HEAD_DIM)
            pv = jnp.dot(vt_ref[0, j, ch, :], p3.reshape(tk, tq).astype(BF16),
                         preferred_element_type=F32)
            acc3 = acc_ref[ch, :].reshape(HEAD_DIM // SUBLANES, SUBLANES, tq)
            acc_ref[ch, :] = (acc3 * alpha[None]).reshape(HEAD_DIM, tq) + pv

    for h in range(QK_LOOKAHEAD):
        st_ref[h] = logits_t(0, h)

    def body(j, carry):
        step(j, False)
        return carry

    lax.fori_loop(0, qi, body, 0)
    step(qi, True)

    outs = []
    for h in range(HEADS):
        stat = slice(h * SUBLANES, (h + 1) * SUBLANES)
        ch = slice(h * HEAD_DIM, (h + 1) * HEAD_DIM)
        l_tot = _sublane_allreduce(l_ref[stat, :], jnp.add)
        acc3 = acc_ref[ch, :].reshape(HEAD_DIM // SUBLANES, SUBLANES, tq)
        outs.append((acc3 / l_tot[None]).reshape(HEAD_DIM, tq))
    o_ref[...] = jnp.concatenate(outs, axis=0).T.astype(BF16)


def _attention(qt, k, ka, vt, fqt, batch, seq):
    tq, tk = TQ, TK
    assert tq == tk
    nq = seq // tq
    w = HEADS * HEAD_DIM
    return pl.pallas_call(
        functools.partial(_attn_kernel, tq=tq, tk=tk),
        grid=(batch, nq),
        in_specs=[pl.BlockSpec((1, 1, w, tq), lambda b, i: (b, i, 0, 0)),
                  pl.BlockSpec((seq, w), lambda b, i: (b, 0)),
                  pl.BlockSpec((seq, w), lambda b, i: (b, 0)),
                  pl.BlockSpec((1, seq // tk, w, tk), lambda b, i: (b, 0, 0, 0)),
                  pl.BlockSpec((1, HEADS, tq), lambda b, i: (b, 0, i))],
        out_specs=pl.BlockSpec((tq, w), lambda b, i: (b * nq + i, 0)),
        out_shape=jax.ShapeDtypeStruct(k.shape, BF16),
        scratch_shapes=[pltpu.VMEM((HEADS, 2 * LANES, tq), BF16),
                        pltpu.VMEM((QK_LOOKAHEAD, tk, tq), F32),
                        pltpu.VMEM((HEADS * SUBLANES, tq), F32),
                        pltpu.VMEM((HEADS * SUBLANES, tq), F32),
                        pltpu.VMEM((w, tq), F32)],
        compiler_params=_params(("parallel", "arbitrary")),
        name="fox_attention",
    )(qt, k, ka, vt, fqt)


def _mix_kernel(a_ref, u_ref, up_ref, cv_ref, cvp_ref, cc_ref, ccp_ref, cb_ref, g_ref, x_ref,
                bg_ref, wpa_ref, pw_ref, ps_ref, wpp_ref, cw_ref, wpc_ref, wo_ref,
                o_ref, ue_ref, ze_ref, *, tm, tiles_per_seq):
    i = pl.program_id(0) % tiles_per_seq
    has_prev = i != 0

    u = u_ref[...].astype(F32)
    ue_ref[0:HALO, :] = jnp.where(has_prev, up_ref[...].astype(F32), 0.0)
    ue_ref[HALO:, :] = u
    z_prev = cvp_ref[...].astype(F32) * ccp_ref[...].astype(F32)
    ze_ref[0:HALO, :] = jnp.where(has_prev, z_prev, 0.0)
    ze_ref[HALO:, :] = cv_ref[...].astype(F32) * cc_ref[...].astype(F32)

    y_a = jnp.dot(a_ref[...], wpa_ref[...], preferred_element_type=F32)

    pos = i * tm + lax.broadcasted_iota(jnp.int32, (tm, 1), 0)
    ys = []
    for gi, w in enumerate(POOL_WINDOWS):
        cs = slice(gi * POOL_GROUP_DIM, (gi + 1) * POOL_GROUP_DIM)
        acc = u[:, cs]
        for lag in range(1, w):
            acc = acc + ue_ref[HALO - lag:HALO - lag + tm, cs]
        n_avail = jnp.minimum(pos + 1, w).astype(F32)
        d = acc / n_avail - u[:, cs]
        y = jnp.dot(d.astype(BF16), pw_ref[gi], preferred_element_type=F32)
        ys.append((y * ps_ref[:, cs]).astype(BF16))
    y_b = jnp.dot(jnp.concatenate(ys, axis=1), wpp_ref[...], preferred_element_type=F32)

    conv = cw_ref[CONV_K - 1:CONV_K, :] * ze_ref[HALO:, :]
    for lag in range(1, CONV_K):
        conv = conv + cw_ref[CONV_K - 1 - lag:CONV_K - lag, :] * ze_ref[HALO - lag:HALO - lag + tm, :]
    y_c = jnp.dot((cb_ref[...].astype(F32) * conv).astype(BF16), wpc_ref[...],
                  preferred_element_type=F32)

    mixed = None
    for n, y in enumerate((y_a, y_b, y_c)):
        cs = slice(n * D_MODEL, (n + 1) * D_MODEL)
        gate = jax.nn.sigmoid(g_ref[:, cs].astype(F32) + bg_ref[:, cs])
        mixed = gate * y if mixed is None else mixed + gate * y
    o_ref[...] = x_ref[...] + jnp.dot(mixed.astype(BF16), wo_ref[...], preferred_element_type=F32)


def _mix(a, u, cv, cc, cb, g, x2, bg, wpa, pw, ps, wpp, cw, wpc, wo, seq):
    n = x2.shape[0]
    tm = TM_MIX
    tiles_per_seq = seq // tm
    row = lambda c: pl.BlockSpec((tm, c), lambda i: (i, 0))
    prev = pl.BlockSpec((HALO, BRANCH_W), lambda i: (jnp.maximum(i * (tm // HALO) - 1, 0), 0))
    return pl.pallas_call(
        functools.partial(_mix_kernel, tm=tm, tiles_per_seq=tiles_per_seq),
        grid=(n // tm,),
        in_specs=[row(BRANCH_W), row(BRANCH_W), prev, row(BRANCH_W), prev, row(BRANCH_W), prev,
                  row(BRANCH_W), row(3 * D_MODEL), row(D_MODEL)]
                 + [_const_spec(t.shape) for t in (bg, wpa, pw, ps, wpp, cw, wpc, wo)],
        out_specs=row(D_MODEL),
        out_shape=jax.ShapeDtypeStruct(x2.shape, F32),
        scratch_shapes=[pltpu.VMEM((tm + HALO, BRANCH_W), F32),
                        pltpu.VMEM((tm + HALO, BRANCH_W), F32)],
        compiler_params=_params(("parallel",)),
        name="branch_mix",
    )(a, u, u, cv, cv, cc, cc, cb, g, x2, bg, wpa, pw, ps, wpp, cw, wpc, wo)


def _ffn_kernel(x_ref, nw_ref, wg_ref, wu_ref, wd_ref, fn_ref, o_ref, *, final):
    x = x_ref[...]
    h = _rms(x, nw_ref[...]).astype(BF16)
    acc = x
    for lo, hi in FFN_CHUNKS:
        a = jnp.dot(h, wg_ref[:, lo:hi], preferred_element_type=F32)
        b = jnp.dot(h, wu_ref[:, lo:hi], preferred_element_type=F32)
        act = (a * jax.nn.sigmoid(a) * b).astype(BF16)
        acc = acc + jnp.dot(act, wd_ref[lo:hi, :], preferred_element_type=F32)
    if final:
        acc = _rms(acc, fn_ref[...])
    o_ref[...] = acc


def _ffn(x2, nw, wg, wu, wd, fn, final):
    n = x2.shape[0]
    tm = TM_FFN
    row = pl.BlockSpec((tm, D_MODEL), lambda i: (i, 0))
    return pl.pallas_call(
        functools.partial(_ffn_kernel, final=final),
        grid=(n // tm,),
        in_specs=[row] + [_const_spec(t.shape) for t in (nw, wg, wu, wd, fn)],
        out_specs=row,
        out_shape=jax.ShapeDtypeStruct(x2.shape, F32),
        compiler_params=_params(("parallel",)),
        name="swiglu_ffn",
    )(x2, nw, wg, wu, wd, fn)


def kernel(x, attn_norm, w_in, b_forget, b_gate, w_proj_attn, pool_w, pool_scale, w_proj_pool,
           conv_w, w_proj_conv, w_out, ffn_norm, w_gate_up, w_down, final_norm):
    batch, seq, d = x.shape
    depth = w_in.shape[0]
    x2 = x.reshape(batch * seq, d)
    qkv_w = 3 * BRANCH_W
    fn = final_norm.reshape(1, d)
    q_scale = LOG2E * HEAD_DIM ** -0.5
    for l in range(depth):
        wl = w_in[l]
        w_main = jnp.concatenate([wl[:, :BRANCH_W] * q_scale, wl[:, BRANCH_W:qkv_w],
                                  wl[:, qkv_w + HEADS:]], axis=1).astype(BF16)
        w_f = jnp.pad(wl[:, qkv_w:qkv_w + HEADS], ((0, 0), (0, LANES - HEADS))).astype(BF16)
        b_f = jnp.pad(b_forget[l], (0, LANES - HEADS)).reshape(1, LANES)
        qt, k, vt, u, cv, cb, cc, g, lf = _inproj(x2, attn_norm[l].reshape(1, d), w_main, w_f, b_f,
                                                  batch, seq)
        fqt, ka = _cumsum(lf, batch, seq)
        a = _attention(qt, k, ka, vt, fqt, batch, seq)

        x2 = _mix(a, u, cv, cc, cb, g, x2, b_gate[l].reshape(1, -1),
                  w_proj_attn[l].astype(BF16), pool_w[l].astype(BF16),
                  pool_scale[l].reshape(1, -1), w_proj_pool[l].astype(BF16),
                  conv_w[l], w_proj_conv[l].astype(BF16), w_out[l].astype(BF16), seq)

        x2 = _ffn(x2, ffn_norm[l].reshape(1, d), w_gate_up[l][:, :FFN_HIDDEN].astype(BF16),
                  w_gate_up[l][:, FFN_HIDDEN:].astype(BF16), w_down[l].astype(BF16), fn,
                  final=(l == depth - 1))
    return x2.reshape(batch, seq, d)
```

```python
import functools
import math

import jax
import jax.numpy as jnp
from jax import lax
from jax.experimental import pallas as pl
from jax.experimental.pallas import tpu as pltpu

D_MODEL = 1024
HEADS = 8
HEAD_DIM = 64
BRANCH_W = 512
POOL_WINDOWS = (2, 4, 8, 16)
POOL_GROUP_DIM = 128
CONV_K = 3
FFN_HIDDEN = 2816
FFN_CHUNKS = ((0, 1536), (1536, 2816))
RMS_EPS = 1e-6
NEG_INF = -1e30
LOG2E = math.log2(math.e)
LANES = 128
SUBLANES = 8
HALO = 16
HEADS_PER_SLAB = LANES // HEAD_DIM
F_TERMS = 3

BF16 = jnp.bfloat16
F32 = jnp.float32

TM_PROJ = 512
TM_MIX = 512
TM_FFN = 512
TQ = 512
TK = 512
CUMSUM_CHUNK = 256
QK_LOOKAHEAD = 4
QK_ISSUE_SLOTS = (0, 1, 2, 2, 3, 3, 4, 5)
ACC_ROWS = HEAD_DIM + SUBLANES
ACC_ROWS_PADDED = HEAD_DIM + 16
VMEM_LIMIT = 56 * 1024 * 1024


def _rms(x, w):
    return x * lax.rsqrt(jnp.mean(x * x, axis=-1, keepdims=True) + RMS_EPS) * w


def _const_spec(shape):
    nd = len(shape)
    return pl.BlockSpec(shape, lambda *_: (0,) * nd, pipeline_mode=pl.Buffered(1))


def _params(sem):
    return pltpu.CompilerParams(dimension_semantics=sem, vmem_limit_bytes=VMEM_LIMIT)


def _inproj_kernel(x_ref, nw_ref, w_ref, wf_ref, bf_ref,
                   qt_ref, k_ref, vt_ref, u_ref, cv_ref, cb_ref, cc_ref, g_ref, lf_ref):
    h = _rms(x_ref[...], nw_ref[...]).astype(BF16)

    def proj(n):
        return jnp.dot(h, w_ref[:, n * BRANCH_W:(n + 1) * BRANCH_W], preferred_element_type=F32)

    qt = proj(0).T.astype(BF16)
    for c in range(qt_ref.shape[1]):
        qt_ref[0, c] = qt[:, c * TQ:(c + 1) * TQ]
    k_ref[...] = proj(1).astype(BF16)
    vt = proj(2).T.astype(BF16)
    for c in range(vt_ref.shape[1]):
        vt_ref[0, c] = vt[:, c * TK:(c + 1) * TK]
    for n, o_ref in enumerate((u_ref, cv_ref, cb_ref, cc_ref), start=3):
        o_ref[...] = proj(n).astype(BF16)
    for n in range(3 * D_MODEL // BRANCH_W):
        g_ref[:, n * BRANCH_W:(n + 1) * BRANCH_W] = proj(7 + n).astype(BF16)
    f = jnp.dot(h, wf_ref[...], preferred_element_type=F32) + bf_ref[...]
    lf_ref[...] = jnp.minimum(f, 0.0) - jnp.log1p(jnp.exp(-jnp.abs(f)))


def _inproj(x2, nw, w_main, w_f, b_f, batch, seq):
    n = x2.shape[0]
    tm = TM_PROJ
    tiles_per_seq = seq // tm
    row = lambda c: pl.BlockSpec((tm, c), lambda i: (i, 0))
    assert TQ == TK
    vt_spec = pl.BlockSpec((1, tm // TK, BRANCH_W, TK),
                           lambda i: (i // tiles_per_seq, i % tiles_per_seq, 0, 0))
    act = jax.ShapeDtypeStruct((n, BRANCH_W), BF16)
    act_t = jax.ShapeDtypeStruct((batch, seq // TK, BRANCH_W, TK), BF16)
    outs = [act_t, act, act_t, act, act, act, act,
            jax.ShapeDtypeStruct((n, 3 * D_MODEL), BF16), jax.ShapeDtypeStruct((n, LANES), F32)]
    return pl.pallas_call(
        _inproj_kernel,
        grid=(n // tm,),
        in_specs=[row(D_MODEL), _const_spec(nw.shape), _const_spec(w_main.shape),
                  _const_spec(w_f.shape), _const_spec(b_f.shape)],
        out_specs=[vt_spec, row(BRANCH_W), vt_spec] + [row(BRANCH_W)] * 4
                  + [row(3 * D_MODEL), row(LANES)],
        out_shape=outs,
        compiler_params=_params(("parallel",)),
        name="inproj",
    )(x2, nw, w_main, w_f, b_f)


def _cumsum_kernel(lf_ref, fqt_ref, ka_ref):
    c = CUMSUM_CHUNK
    r = lax.broadcasted_iota(jnp.int32, (c, c), 0)
    s = lax.broadcasted_iota(jnp.int32, (c, c), 1)
    tri = (s <= r).astype(BF16)

    src = lax.broadcasted_iota(jnp.int32, (F_TERMS * LANES, BRANCH_W), 0)
    dst = lax.broadcasted_iota(jnp.int32, (F_TERMS * LANES, BRANCH_W), 1)
    head, term = src % LANES, src // LANES
    want = (head // HEADS_PER_SLAB) * LANES + F_TERMS * (head % HEADS_PER_SLAB) + term
    place = ((head < HEADS) & (dst == want)).astype(BF16)

    def split(x):
        hi = x.astype(BF16)
        r1 = x - hi.astype(F32)
        mid = r1.astype(BF16)
        lo = (r1 - mid.astype(F32)).astype(BF16)
        return jnp.concatenate([hi, mid, lo], axis=1)

    carry = jnp.zeros((1, LANES), F32)
    for n in range(lf_ref.shape[0] // c):
        rows = slice(n * c, (n + 1) * c)
        parts = jnp.dot(tri, split(lf_ref[rows, :]), preferred_element_type=F32)
        cum = parts[:, :LANES] + parts[:, LANES:2 * LANES] + parts[:, 2 * LANES:] + carry
        carry = cum[c - 1:c, :]
        f2 = cum * LOG2E
        fqt_ref[0, :, rows] = f2.T[:HEADS, :]
        ka_ref[rows, :] = jnp.dot(split(-f2), place, preferred_element_type=F32).astype(BF16)


def _cumsum(lf, batch, seq):
    return pl.pallas_call(
        _cumsum_kernel,
        grid=(batch,),
        in_specs=[pl.BlockSpec((seq, LANES), lambda b: (b, 0))],
        out_specs=[pl.BlockSpec((1, HEADS, seq), lambda b: (b, 0, 0)),
                   pl.BlockSpec((seq, BRANCH_W), lambda b: (b, 0))],
        out_shape=[jax.ShapeDtypeStruct((batch, HEADS, seq), F32),
                   jax.ShapeDtypeStruct((batch * seq, BRANCH_W), BF16)],
        compiler_params=_params(("parallel",)),
        name="forget_cumsum",
    )(lf)


def _sublane_allreduce(x, op):
    shift = SUBLANES // 2
    while shift:
        x = op(x, pltpu.roll(x, shift, axis=0))
        shift //= 2
    return x


def _attn_kernel(qt_ref, k_ref, ka_ref, vt_ref, fq_ref, o_ref,
                 qa_ref, st_ref, m_ref, acc_ref, *, tq, tk):
    qi = pl.program_id(1)
    groups = tk // SUBLANES
    chan = lax.broadcasted_iota(jnp.int32, (LANES, 1), 0)

    def build_query(h):
        slab, e = divmod(h, HEADS_PER_SLAB)
        q2 = qt_ref[0, 0, slab * LANES:(slab + 1) * LANES, :]
        qm = jnp.where(chan // HEAD_DIM == e, q2, jnp.zeros_like(q2))
        ones = ((chan >= F_TERMS * e) & (chan < F_TERMS * (e + 1))).astype(BF16)
        qa_ref[h] = jnp.concatenate([qm, jnp.broadcast_to(ones, (LANES, tq))], axis=0)

    def logits_t(j, h):
        rows = pl.ds(pl.multiple_of(j * tk, tk), tk)
        cols = slice(h // HEADS_PER_SLAB * LANES, (h // HEADS_PER_SLAB + 1) * LANES)
        lhs = jnp.concatenate([k_ref[rows, cols], ka_ref[rows, cols]], axis=1)
        return jnp.dot(lhs, qa_ref[h], preferred_element_type=F32)

    for h in range(HEADS):
        build_query(h)
        if h < QK_LOOKAHEAD:
            st_ref[h] = logits_t(0, h)
    m_ref[...] = jnp.full(m_ref.shape, NEG_INF, F32)
    acc_ref[...] = jnp.zeros(acc_ref.shape, F32)

    key_pos = lax.broadcasted_iota(jnp.int32, (tk, tq), 0)
    query_pos = lax.broadcasted_iota(jnp.int32, (tk, tq), 1)
    causal = key_pos <= query_pos
    ones_rows = jnp.ones((ACC_ROWS_PADDED - HEAD_DIM, tk), BF16)

    def step(j, last):
        issues = [(False, h) for h in range(QK_LOOKAHEAD, HEADS)]
        if not last:
            issues += [(True, h) for h in range(QK_LOOKAHEAD)]
        pending = {}
        for h in range(HEADS):
            for (for_next, hh), slot in zip(issues, QK_ISSUE_SLOTS):
                if slot == h and for_next:
                    st_ref[hh] = logits_t(j + 1, hh)
                elif slot == h:
                    pending[hh] = logits_t(j, hh)
            st = st_ref[h] if h < QK_LOOKAHEAD else pending.pop(h)
            if last:
                st = jnp.where(causal, st, NEG_INF)
            st3 = st.reshape(groups, SUBLANES, tq)
            stat = slice(h * SUBLANES, (h + 1) * SUBLANES)
            fq = jnp.broadcast_to(fq_ref[0, h:h + 1, :], (SUBLANES, tq))
            m_old = m_ref[stat, :]
            m_new = jnp.maximum(m_old, _sublane_allreduce(jnp.max(st3, axis=0), jnp.maximum) + fq)
            p = jnp.exp2(st3 - (m_new - fq)[None]).reshape(tk, tq).astype(BF16)
            alpha = jnp.exp2(m_old - m_new)
            m_ref[stat, :] = m_new
            v_ones = jnp.concatenate([vt_ref[0, j, h * HEAD_DIM:(h + 1) * HEAD_DIM, :], ones_rows],
                                     axis=0)
            pv = jnp.dot(v_ones, p, preferred_element_type=F32)[:ACC_ROWS]
            acc3 = acc_ref[h].reshape(ACC_ROWS // SUBLANES, SUBLANES, tq)
            acc_ref[h] = (acc3 * alpha[None]).reshape(ACC_ROWS, tq) + pv

    def body(j, carry):
        step(j, False)
        return carry

    lax.fori_loop(0, qi, body, 0)
    step(qi, True)

    outs = []
    for h in range(HEADS):
        acc3 = acc_ref[h].reshape(ACC_ROWS // SUBLANES, SUBLANES, tq)
        outs.append((acc3[:HEAD_DIM // SUBLANES] / acc3[HEAD_DIM // SUBLANES][None])
                    .reshape(HEAD_DIM, tq))
    o_ref[...] = jnp.concatenate(outs, axis=0).T.astype(BF16)


def _attention(qt, k, ka, vt, fqt, batch, seq):
    tq, tk = TQ, TK
    assert tq == tk
    nq = seq // tq
    w = HEADS * HEAD_DIM
    return pl.pallas_call(
        functools.partial(_attn_kernel, tq=tq, tk=tk),
        grid=(batch, nq),
        in_specs=[pl.BlockSpec((1, 1, w, tq), lambda b, i: (b, i, 0, 0)),
                  pl.BlockSpec((seq, w), lambda b, i: (b, 0)),
                  pl.BlockSpec((seq, w), lambda b, i: (b, 0)),
                  pl.BlockSpec((1, seq // tk, w, tk), lambda b, i: (b, 0, 0, 0)),
                  pl.BlockSpec((1, HEADS, tq), lambda b, i: (b, 0, i))],
        out_specs=pl.BlockSpec((tq, w), lambda b, i: (b * nq + i, 0)),
        out_shape=jax.ShapeDtypeStruct(k.shape, BF16),
        scratch_shapes=[pltpu.VMEM((HEADS, 2 * LANES, tq), BF16),
                        pltpu.VMEM((QK_LOOKAHEAD, tk, tq), F32),
                        pltpu.VMEM((HEADS * SUBLANES, tq), F32),
                        pltpu.VMEM((HEADS, ACC_ROWS, tq), F32)],
        compiler_params=_params(("parallel", "arbitrary")),
        name="fox_attention",
    )(qt, k, ka, vt, fqt)


def _mix_kernel(a_ref, u_ref, up_ref, cv_ref, cvp_ref, cc_ref, ccp_ref, cb_ref, g_ref, x_ref,
                bg_ref, wpa_ref, pw_ref, ps_ref, wpp_ref, cw_ref, wpc_ref, wo_ref,
                o_ref, ue_ref, ze_ref, *, tm, tiles_per_seq):
    i = pl.program_id(0) % tiles_per_seq
    has_prev = i != 0
    pad, first = SUBLANES, SUBLANES + HALO

    u = u_ref[...].astype(F32)
    ue_ref[0:pad, :] = jnp.zeros((pad, BRANCH_W), F32)
    ue_ref[pad:first, :] = jnp.where(has_prev, up_ref[...].astype(F32), 0.0)
    ue_ref[first:, :] = u
    z_prev = cvp_ref[...].astype(F32) * ccp_ref[...].astype(F32)
    ze_ref[0:HALO, :] = jnp.where(has_prev, z_prev, 0.0)
    ze_ref[HALO:, :] = cv_ref[...].astype(F32) * cc_ref[...].astype(F32)

    y_a = jnp.dot(a_ref[...], wpa_ref[...], preferred_element_type=F32)

    for level in range(len(POOL_WINDOWS) - 1):
        lag = 2 ** level
        cs = slice(level * POOL_GROUP_DIM, BRANCH_W)
        ue_ref[pad:, cs] = ue_ref[pad:, cs] + ue_ref[pad - lag:pad - lag + tm + HALO, cs]
    pos = i * tm + lax.broadcasted_iota(jnp.int32, (tm, 1), 0)
    ys = []
    for gi, w in enumerate(POOL_WINDOWS):
        cs = slice(gi * POOL_GROUP_DIM, (gi + 1) * POOL_GROUP_DIM)
        win = ue_ref[first:, cs]
        if gi == len(POOL_WINDOWS) - 1:
            win = win + ue_ref[first - w // 2:first - w // 2 + tm, cs]
        n_avail = jnp.minimum(pos + 1, w).astype(F32)
        d = win / n_avail - u[:, cs]
        y = jnp.dot(d.astype(BF16), pw_ref[gi], preferred_element_type=F32)
        ys.append((y * ps_ref[:, cs]).astype(BF16))
    y_b = jnp.dot(jnp.concatenate(ys, axis=1), wpp_ref[...], preferred_element_type=F32)

    conv = cw_ref[CONV_K - 1:CONV_K, :] * ze_ref[HALO:, :]
    for lag in range(1, CONV_K):
        conv = conv + cw_ref[CONV_K - 1 - lag:CONV_K - lag, :] * ze_ref[HALO - lag:HALO - lag + tm, :]
    y_c = jnp.dot((cb_ref[...].astype(F32) * conv).astype(BF16), wpc_ref[...],
                  preferred_element_type=F32)

    mixed = None
    for n, y in enumerate((y_a, y_b, y_c)):
        cs = slice(n * D_MODEL, (n + 1) * D_MODEL)
        gate2 = 1.0 + jnp.tanh(g_ref[:, cs].astype(F32) + bg_ref[:, cs])
        mixed = gate2 * y if mixed is None else mixed + gate2 * y
    o_ref[...] = x_ref[...] + jnp.dot(mixed.astype(BF16), wo_ref[...], preferred_element_type=F32)


def _mix(a, u, cv, cc, cb, g, x2, bg, wpa, pw, ps, wpp, cw, wpc, wo, seq):
    n = x2.shape[0]
    tm = TM_MIX
    tiles_per_seq = seq // tm
    row = lambda c: pl.BlockSpec((tm, c), lambda i: (i, 0))
    prev = pl.BlockSpec((HALO, BRANCH_W), lambda i: (jnp.maximum(i * (tm // HALO) - 1, 0), 0))
    return pl.pallas_call(
        functools.partial(_mix_kernel, tm=tm, tiles_per_seq=tiles_per_seq),
        grid=(n // tm,),
        in_specs=[row(BRANCH_W), row(BRANCH_W), prev, row(BRANCH_W), prev, row(BRANCH_W), prev,
                  row(BRANCH_W), row(3 * D_MODEL), row(D_MODEL)]
                 + [_const_spec(t.shape) for t in (bg, wpa, pw, ps, wpp, cw, wpc, wo)],
        out_specs=row(D_MODEL),
        out_shape=jax.ShapeDtypeStruct(x2.shape, F32),
        scratch_shapes=[pltpu.VMEM((SUBLANES + HALO + tm, BRANCH_W), F32),
                        pltpu.VMEM((HALO + tm, BRANCH_W), F32)],
        compiler_params=_params(("parallel",)),
        name="branch_mix",
    )(a, u, u, cv, cv, cc, cc, cb, g, x2, bg, wpa, pw, ps, wpp, cw, wpc, wo)


def _ffn_kernel(x_ref, nw_ref, wg_ref, wu_ref, wd_ref, fn_ref, o_ref, *, final):
    x = x_ref[...]
    h = _rms(x, nw_ref[...]).astype(BF16)
    acc = x
    for lo, hi in FFN_CHUNKS:
        a = jnp.dot(h, wg_ref[:, lo:hi], preferred_element_type=F32)
        b = jnp.dot(h, wu_ref[:, lo:hi], preferred_element_type=F32)
        act = (a * jax.nn.sigmoid(a) * b).astype(BF16)
        acc = acc + jnp.dot(act, wd_ref[lo:hi, :], preferred_element_type=F32)
    if final:
        acc = _rms(acc, fn_ref[...])
    o_ref[...] = acc


def _ffn(x2, nw, wg, wu, wd, fn, final):
    n = x2.shape[0]
    tm = TM_FFN
    row = pl.BlockSpec((tm, D_MODEL), lambda i: (i, 0))
    return pl.pallas_call(
        functools.partial(_ffn_kernel, final=final),
        grid=(n // tm,),
        in_specs=[row] + [_const_spec(t.shape) for t in (nw, wg, wu, wd, fn)],
        out_specs=row,
        out_shape=jax.ShapeDtypeStruct(x2.shape, F32),
        compiler_params=_params(("parallel",)),
        name="swiglu_ffn",
    )(x2, nw, wg, wu, wd, fn)


def kernel(x, attn_norm, w_in, b_forget, b_gate, w_proj_attn, pool_w, pool_scale, w_proj_pool,
           conv_w, w_proj_conv, w_out, ffn_norm, w_gate_up, w_down, final_norm):
    batch, seq, d = x.shape
    depth = w_in.shape[0]
    x2 = x.reshape(batch * seq, d)
    qkv_w = 3 * BRANCH_W
    gate_lo = qkv_w + HEADS + 4 * BRANCH_W
    w_main = jnp.concatenate([w_in[:, :, :BRANCH_W] * (LOG2E * HEAD_DIM ** -0.5),
                              w_in[:, :, BRANCH_W:qkv_w],
                              w_in[:, :, qkv_w + HEADS:gate_lo],
                              w_in[:, :, gate_lo:] * 0.5], axis=2).astype(BF16)
    w_f = jnp.pad(w_in[:, :, qkv_w:qkv_w + HEADS], ((0, 0), (0, 0), (0, LANES - HEADS))).astype(BF16)
    b_f = jnp.pad(b_forget, ((0, 0), (0, LANES - HEADS))).reshape(depth, 1, LANES)
    b_g = (b_gate * 0.5).reshape(depth, 1, -1)
    w_pa, w_pp, w_pc = (w.astype(BF16) for w in (w_proj_attn, w_proj_pool, w_proj_conv))
    w_o = (w_out * 0.5).astype(BF16)
    w_pool = pool_w.astype(BF16)
    w_g = w_gate_up[:, :, :FFN_HIDDEN].astype(BF16)
    w_u = w_gate_up[:, :, FFN_HIDDEN:].astype(BF16)
    w_d = w_down.astype(BF16)
    a_norm, f_norm = attn_norm.reshape(depth, 1, d), ffn_norm.reshape(depth, 1, d)
    p_scale = pool_scale.reshape(depth, 1, -1)
    fn = final_norm.reshape(1, d)
    for l in range(depth):
        qt, k, vt, u, cv, cb, cc, g, lf = _inproj(x2, a_norm[l], w_main[l], w_f[l], b_f[l],
                                                  batch, seq)
        fqt, ka = _cumsum(lf, batch, seq)
        a = _attention(qt, k, ka, vt, fqt, batch, seq)
        x2 = _mix(a, u, cv, cc, cb, g, x2, b_g[l], w_pa[l], w_pool[l], p_scale[l], w_pp[l],
                  conv_w[l], w_pc[l], w_o[l], seq)
        x2 = _ffn(x2, f_norm[l], w_g[l], w_u[l], w_d[l], fn, final=(l == depth - 1))
    return x2.reshape(batch, seq, d)
```

```python
import functools
import math

import jax
import jax.numpy as jnp
from jax import lax
from jax.experimental import pallas as pl
from jax.experimental.pallas import tpu as pltpu

D_MODEL = 1024
HEADS = 8
HEAD_DIM = 64
BRANCH_W = 512
POOL_WINDOWS = (2, 4, 8, 16)
POOL_GROUP_DIM = 128
CONV_K = 3
FFN_HIDDEN = 2816
FFN_CHUNKS = ((0, 1536), (1536, 2816))
RMS_EPS = 1e-6
NEG_INF = -1e30
LOG2E = math.log2(math.e)
LANES = 128
SUBLANES = 8
HALO = 16
HEADS_PER_SLAB = LANES // HEAD_DIM
F_TERMS = 3

BF16 = jnp.bfloat16
F32 = jnp.float32

TM_PROJ = 512
TM_MIX = 512
TM_FFN = 512
TQ = 512
TK = 512
CUMSUM_CHUNK = 256
QK_LOOKAHEAD = 4
ACC_ROWS = HEAD_DIM + SUBLANES
ACC_ROWS_PADDED = HEAD_DIM + 16
VMEM_LIMIT = 56 * 1024 * 1024


def _rms(x, w):
    return x * lax.rsqrt(jnp.mean(x * x, axis=-1, keepdims=True) + RMS_EPS) * w


def _const_spec(shape):
    nd = len(shape)
    return pl.BlockSpec(shape, lambda *_: (0,) * nd, pipeline_mode=pl.Buffered(1))


def _params(sem):
    return pltpu.CompilerParams(dimension_semantics=sem, vmem_limit_bytes=VMEM_LIMIT)


def _inproj_kernel(x_ref, nw_ref, w_ref, wf_ref, bf_ref,
                   qt_ref, k_ref, vt_ref, u_ref, cv_ref, cb_ref, cc_ref, g_ref, lf_ref):
    h = _rms(x_ref[...], nw_ref[...]).astype(BF16)

    def proj(n):
        return jnp.dot(h, w_ref[:, n * BRANCH_W:(n + 1) * BRANCH_W], preferred_element_type=F32)

    qt = proj(0).T.astype(BF16)
    for c in range(qt_ref.shape[1]):
        qt_ref[0, c] = qt[:, c * TQ:(c + 1) * TQ]
    k_ref[...] = proj(1).astype(BF16)
    vt = proj(2).T.astype(BF16)
    for c in range(vt_ref.shape[1]):
        vt_ref[0, c] = vt[:, c * TK:(c + 1) * TK]
    for n, o_ref in enumerate((u_ref, cv_ref, cb_ref, cc_ref), start=3):
        o_ref[...] = proj(n).astype(BF16)
    for n in range(3 * D_MODEL // BRANCH_W):
        g_ref[:, n * BRANCH_W:(n + 1) * BRANCH_W] = proj(7 + n).astype(BF16)
    f = jnp.dot(h, wf_ref[...], preferred_element_type=F32) + bf_ref[...]
    lf_ref[...] = jnp.minimum(f, 0.0) - jnp.log1p(jnp.exp(-jnp.abs(f)))


def _inproj(x2, nw, w_main, w_f, b_f, batch, seq):
    n = x2.shape[0]
    tm = TM_PROJ
    tiles_per_seq = seq // tm
    row = lambda c: pl.BlockSpec((tm, c), lambda i: (i, 0))
    assert TQ == TK
    vt_spec = pl.BlockSpec((1, tm // TK, BRANCH_W, TK),
                           lambda i: (i // tiles_per_seq, i % tiles_per_seq, 0, 0))
    act = jax.ShapeDtypeStruct((n, BRANCH_W), BF16)
    act_t = jax.ShapeDtypeStruct((batch, seq // TK, BRANCH_W, TK), BF16)
    outs = [act_t, act, act_t, act, act, act, act,
            jax.ShapeDtypeStruct((n, 3 * D_MODEL), BF16), jax.ShapeDtypeStruct((n, LANES), F32)]
    return pl.pallas_call(
        _inproj_kernel,
        grid=(n // tm,),
        in_specs=[row(D_MODEL), _const_spec(nw.shape), _const_spec(w_main.shape),
                  _const_spec(w_f.shape), _const_spec(b_f.shape)],
        out_specs=[vt_spec, row(BRANCH_W), vt_spec] + [row(BRANCH_W)] * 4
                  + [row(3 * D_MODEL), row(LANES)],
        out_shape=outs,
        compiler_params=_params(("parallel",)),
        name="inproj",
    )(x2, nw, w_main, w_f, b_f)


def _cumsum_kernel(lf_ref, fqt_ref, ka_ref):
    c = CUMSUM_CHUNK
    r = lax.broadcasted_iota(jnp.int32, (c, c), 0)
    s = lax.broadcasted_iota(jnp.int32, (c, c), 1)
    tri = (s <= r).astype(BF16)

    src = lax.broadcasted_iota(jnp.int32, (F_TERMS * LANES, BRANCH_W), 0)
    dst = lax.broadcasted_iota(jnp.int32, (F_TERMS * LANES, BRANCH_W), 1)
    head, term = src % LANES, src // LANES
    want = (head // HEADS_PER_SLAB) * LANES + F_TERMS * (head % HEADS_PER_SLAB) + term
    place = ((head < HEADS) & (dst == want)).astype(BF16)

    def split(x):
        hi = x.astype(BF16)
        r1 = x - hi.astype(F32)
        mid = r1.astype(BF16)
        lo = (r1 - mid.astype(F32)).astype(BF16)
        return jnp.concatenate([hi, mid, lo], axis=1)

    carry = jnp.zeros((1, LANES), F32)
    for n in range(lf_ref.shape[0] // c):
        rows = slice(n * c, (n + 1) * c)
        parts = jnp.dot(tri, split(lf_ref[rows, :]), preferred_element_type=F32)
        cum = parts[:, :LANES] + parts[:, LANES:2 * LANES] + parts[:, 2 * LANES:] + carry
        carry = cum[c - 1:c, :]
        f2 = cum * LOG2E
        fqt_ref[0, :, rows] = f2.T[:HEADS, :]
        ka_ref[rows, :] = jnp.dot(split(-f2), place, preferred_element_type=F32).astype(BF16)


def _cumsum(lf, batch, seq):
    return pl.pallas_call(
        _cumsum_kernel,
        grid=(batch,),
        in_specs=[pl.BlockSpec((seq, LANES), lambda b: (b, 0))],
        out_specs=[pl.BlockSpec((1, HEADS, seq), lambda b: (b, 0, 0)),
                   pl.BlockSpec((seq, BRANCH_W), lambda b: (b, 0))],
        out_shape=[jax.ShapeDtypeStruct((batch, HEADS, seq), F32),
                   jax.ShapeDtypeStruct((batch * seq, BRANCH_W), BF16)],
        compiler_params=_params(("parallel",)),
        name="forget_cumsum",
    )(lf)


def _sublane_allreduce(x, op):
    shift = SUBLANES // 2
    while shift:
        x = op(x, pltpu.roll(x, shift, axis=0))
        shift //= 2
    return x


def _attn_kernel(qt_ref, k_ref, ka_ref, vt_ref, fq_ref, o_ref,
                 qa_ref, m_ref, acc_ref, *, tq, tk, nq):
    groups = tk // SUBLANES
    chan = lax.broadcasted_iota(jnp.int32, (LANES, 1), 0)
    key_pos = lax.broadcasted_iota(jnp.int32, (tk, tq), 0)
    query_pos = lax.broadcasted_iota(jnp.int32, (tk, tq), 1)
    causal = key_pos <= query_pos
    ones_rows = jnp.ones((ACC_ROWS_PADDED - HEAD_DIM, tk), BF16)

    def begin_block(qi):
        for h in range(HEADS):
            slab, e = divmod(h, HEADS_PER_SLAB)
            q2 = qt_ref[0, qi, slab * LANES:(slab + 1) * LANES, :]
            qm = jnp.where(chan // HEAD_DIM == e, q2, jnp.zeros_like(q2))
            ones = ((chan >= F_TERMS * e) & (chan < F_TERMS * (e + 1))).astype(BF16)
            qa_ref[qi % 2, h] = jnp.concatenate([qm, jnp.broadcast_to(ones, (LANES, tq))], axis=0)
        m_ref[qi % 2] = jnp.full(m_ref.shape[1:], NEG_INF, F32)
        acc_ref[qi % 2] = jnp.zeros(acc_ref.shape[1:], F32)

    def logits_t(qi, j, h):
        rows = slice(j * tk, (j + 1) * tk)
        cols = slice(h // HEADS_PER_SLAB * LANES, (h // HEADS_PER_SLAB + 1) * LANES)
        lhs = jnp.concatenate([k_ref[rows, cols], ka_ref[rows, cols]], axis=1)
        return jnp.dot(lhs, qa_ref[qi % 2, h], preferred_element_type=F32)

    def softmax_pv(qi, j, h, st):
        if j == qi:
            st = jnp.where(causal, st, NEG_INF)
        st3 = st.reshape(groups, SUBLANES, tq)
        stat = slice(h * SUBLANES, (h + 1) * SUBLANES)
        fq = jnp.broadcast_to(fq_ref[0, h:h + 1, qi * tq:(qi + 1) * tq], (SUBLANES, tq))
        m_old = m_ref[qi % 2, stat, :]
        m_new = jnp.maximum(m_old, _sublane_allreduce(jnp.max(st3, axis=0), jnp.maximum) + fq)
        p = jnp.exp2(st3 - (m_new - fq)[None]).reshape(tk, tq).astype(BF16)
        alpha = jnp.exp2(m_old - m_new)
        m_ref[qi % 2, stat, :] = m_new
        v_ones = jnp.concatenate([vt_ref[0, j, h * HEAD_DIM:(h + 1) * HEAD_DIM, :], ones_rows], axis=0)
        pv = jnp.dot(v_ones, p, preferred_element_type=F32)[:ACC_ROWS]
        acc3 = acc_ref[qi % 2, h].reshape(ACC_ROWS // SUBLANES, SUBLANES, tq)
        acc_ref[qi % 2, h] = (acc3 * alpha[None]).reshape(ACC_ROWS, tq) + pv

    def end_block(qi):
        outs = []
        for h in range(HEADS):
            acc3 = acc_ref[qi % 2, h].reshape(ACC_ROWS // SUBLANES, SUBLANES, tq)
            outs.append((acc3[:HEAD_DIM // SUBLANES] / acc3[HEAD_DIM // SUBLANES][None])
                        .reshape(HEAD_DIM, tq))
        o_ref[qi * tq:(qi + 1) * tq, :] = jnp.concatenate(outs, axis=0).T.astype(BF16)

    items = [(qi, j, h) for qi in range(nq) for j in range(qi + 1) for h in range(HEADS)]
    begun, pending = set(), []

    def issue(n):
        qi = items[n][0]
        if qi not in begun:
            begun.add(qi)
            begin_block(qi)
        pending.append(logits_t(*items[n]))

    for n in range(min(QK_LOOKAHEAD, len(items))):
        issue(n)
    for n, (qi, j, h) in enumerate(items):
        if n + QK_LOOKAHEAD < len(items):
            issue(n + QK_LOOKAHEAD)
        softmax_pv(qi, j, h, pending.pop(0))
        if j == qi and h == HEADS - 1:
            end_block(qi)


def _attention(qt, k, ka, vt, fqt, batch, seq):
    tq, tk = TQ, TK
    assert tq == tk
    nq = seq // tq
    w = HEADS * HEAD_DIM
    return pl.pallas_call(
        functools.partial(_attn_kernel, tq=tq, tk=tk, nq=nq),
        grid=(batch,),
        in_specs=[pl.BlockSpec((1, nq, w, tq), lambda b: (b, 0, 0, 0)),
                  pl.BlockSpec((seq, w), lambda b: (b, 0)),
                  pl.BlockSpec((seq, w), lambda b: (b, 0)),
                  pl.BlockSpec((1, seq // tk, w, tk), lambda b: (b, 0, 0, 0)),
                  pl.BlockSpec((1, HEADS, seq), lambda b: (b, 0, 0))],
        out_specs=pl.BlockSpec((seq, w), lambda b: (b, 0)),
        out_shape=jax.ShapeDtypeStruct(k.shape, BF16),
        scratch_shapes=[pltpu.VMEM((2, HEADS, 2 * LANES, tq), BF16),
                        pltpu.VMEM((2, HEADS * SUBLANES, tq), F32),
                        pltpu.VMEM((2, HEADS, ACC_ROWS, tq), F32)],
        compiler_params=_params(("parallel",)),
        name="fox_attention",
    )(qt, k, ka, vt, fqt)


def _mix_kernel(a_ref, u_ref, up_ref, cv_ref, cvp_ref, cc_ref, ccp_ref, cb_ref, g_ref, x_ref,
                bg_ref, wpa_ref, pw_ref, ps_ref, wpp_ref, cw_ref, wpc_ref, wo_ref,
                o_ref, ue_ref, ze_ref, *, tm, tiles_per_seq):
    i = pl.program_id(0) % tiles_per_seq
    has_prev = i != 0
    pad, first = SUBLANES, SUBLANES + HALO

    u = u_ref[...].astype(F32)
    ue_ref[0:pad, :] = jnp.zeros((pad, BRANCH_W), F32)
    ue_ref[pad:first, :] = jnp.where(has_prev, up_ref[...].astype(F32), 0.0)
    ue_ref[first:, :] = u
    z_prev = cvp_ref[...].astype(F32) * ccp_ref[...].astype(F32)
    ze_ref[0:HALO, :] = jnp.where(has_prev, z_prev, 0.0)
    ze_ref[HALO:, :] = cv_ref[...].astype(F32) * cc_ref[...].astype(F32)

    y_a = jnp.dot(a_ref[...], wpa_ref[...], preferred_element_type=F32)

    for level in range(len(POOL_WINDOWS) - 1):
        lag = 2 ** level
        cs = slice(level * POOL_GROUP_DIM, BRANCH_W)
        ue_ref[pad:, cs] = ue_ref[pad:, cs] + ue_ref[pad - lag:pad - lag + tm + HALO, cs]
    pos = i * tm + lax.broadcasted_iota(jnp.int32, (tm, 1), 0)
    ys = []
    for gi, w in enumerate(POOL_WINDOWS):
        cs = slice(gi * POOL_GROUP_DIM, (gi + 1) * POOL_GROUP_DIM)
        win = ue_ref[first:, cs]
        if gi == len(POOL_WINDOWS) - 1:
            win = win + ue_ref[first - w // 2:first - w // 2 + tm, cs]
        n_avail = jnp.minimum(pos + 1, w).astype(F32)
        d = win / n_avail - u[:, cs]
        y = jnp.dot(d.astype(BF16), pw_ref[gi], preferred_element_type=F32)
        ys.append((y * ps_ref[:, cs]).astype(BF16))
    y_b = jnp.dot(jnp.concatenate(ys, axis=1), wpp_ref[...], preferred_element_type=F32)

    conv = cw_ref[CONV_K - 1:CONV_K, :] * ze_ref[HALO:, :]
    for lag in range(1, CONV_K):
        conv = conv + cw_ref[CONV_K - 1 - lag:CONV_K - lag, :] * ze_ref[HALO - lag:HALO - lag + tm, :]
    y_c = jnp.dot((cb_ref[...].astype(F32) * conv).astype(BF16), wpc_ref[...],
                  preferred_element_type=F32)

    mixed = None
    for n, y in enumerate((y_a, y_b, y_c)):
        cs = slice(n * D_MODEL, (n + 1) * D_MODEL)
        gate2 = 1.0 + jnp.tanh(g_ref[:, cs].astype(F32) + bg_ref[:, cs])
        mixed = gate2 * y if mixed is None else mixed + gate2 * y
    o_ref[...] = x_ref[...] + jnp.dot(mixed.astype(BF16), wo_ref[...], preferred_element_type=F32)


def _mix(a, u, cv, cc, cb, g, x2, bg, wpa, pw, ps, wpp, cw, wpc, wo, seq):
    n = x2.shape[0]
    tm = TM_MIX
    tiles_per_seq = seq // tm
    row = lambda c: pl.BlockSpec((tm, c), lambda i: (i, 0))
    prev = pl.BlockSpec((HALO, BRANCH_W), lambda i: (jnp.maximum(i * (tm // HALO) - 1, 0), 0))
    return pl.pallas_call(
        functools.partial(_mix_kernel, tm=tm, tiles_per_seq=tiles_per_seq),
        grid=(n // tm,),
        in_specs=[row(BRANCH_W), row(BRANCH_W), prev, row(BRANCH_W), prev, row(BRANCH_W), prev,
                  row(BRANCH_W), row(3 * D_MODEL), row(D_MODEL)]
                 + [_const_spec(t.shape) for t in (bg, wpa, pw, ps, wpp, cw, wpc, wo)],
        out_specs=row(D_MODEL),
        out_shape=jax.ShapeDtypeStruct(x2.shape, F32),
        scratch_shapes=[pltpu.VMEM((SUBLANES + HALO + tm, BRANCH_W), F32),
                        pltpu.VMEM((HALO + tm, BRANCH_W), F32)],
        compiler_params=_params(("parallel",)),
        name="branch_mix",
    )(a, u, u, cv, cv, cc, cc, cb, g, x2, bg, wpa, pw, ps, wpp, cw, wpc, wo)


def _ffn_kernel(x_ref, nw_ref, wg_ref, wu_ref, wd_ref, fn_ref, o_ref, *, final):
    x = x_ref[...]
    h = _rms(x, nw_ref[...]).astype(BF16)
    acc = x
    for lo, hi in FFN_CHUNKS:
        a = jnp.dot(h, wg_ref[:, lo:hi], preferred_element_type=F32)
        b = jnp.dot(h, wu_ref[:, lo:hi], preferred_element_type=F32)
        act = (a * jax.nn.sigmoid(a) * b).astype(BF16)
        acc = acc + jnp.dot(act, wd_ref[lo:hi, :], preferred_element_type=F32)
    if final:
        acc = _rms(acc, fn_ref[...])
    o_ref[...] = acc


def _ffn(x2, nw, wg, wu, wd, fn, final):
    n = x2.shape[0]
    tm = TM_FFN
    row = pl.BlockSpec((tm, D_MODEL), lambda i: (i, 0))
    return pl.pallas_call(
        functools.partial(_ffn_kernel, final=final),
        grid=(n // tm,),
        in_specs=[row] + [_const_spec(t.shape) for t in (nw, wg, wu, wd, fn)],
        out_specs=row,
        out_shape=jax.ShapeDtypeStruct(x2.shape, F32),
        compiler_params=_params(("parallel",)),
        name="swiglu_ffn",
    )(x2, nw, wg, wu, wd, fn)


def kernel(x, attn_norm, w_in, b_forget, b_gate, w_proj_attn, pool_w, pool_scale, w_proj_pool,
           conv_w, w_proj_conv, w_out, ffn_norm, w_gate_up, w_down, final_norm):
    batch, seq, d = x.shape
    depth = w_in.shape[0]
    x2 = x.reshape(batch * seq, d)
    qkv_w = 3 * BRANCH_W
    gate_lo = qkv_w + HEADS + 4 * BRANCH_W
    w_main = jnp.concatenate([w_in[:, :, :BRANCH_W] * (LOG2E * HEAD_DIM ** -0.5),
                              w_in[:, :, BRANCH_W:qkv_w],
                              w_in[:, :, qkv_w + HEADS:gate_lo],
                              w_in[:, :, gate_lo:] * 0.5], axis=2).astype(BF16)
    w_f = jnp.pad(w_in[:, :, qkv_w:qkv_w + HEADS], ((0, 0), (0, 0), (0, LANES - HEADS))).astype(BF16)
    b_f = jnp.pad(b_forget, ((0, 0), (0, LANES - HEADS))).reshape(depth, 1, LANES)
    b_g = (b_gate * 0.5).reshape(depth, 1, -1)
    w_pa, w_pp, w_pc = (w.astype(BF16) for w in (w_proj_attn, w_proj_pool, w_proj_conv))
    w_o = (w_out * 0.5).astype(BF16)
    w_pool = pool_w.astype(BF16)
    w_g = w_gate_up[:, :, :FFN_HIDDEN].astype(BF16)
    w_u = w_gate_up[:, :, FFN_HIDDEN:].astype(BF16)
    w_d = w_down.astype(BF16)
    a_norm, f_norm = attn_norm.reshape(depth, 1, d), ffn_norm.reshape(depth, 1, d)
    p_scale = pool_scale.reshape(depth, 1, -1)
    fn = final_norm.reshape(1, d)
    for l in range(depth):
        qt, k, vt, u, cv, cb, cc, g, lf = _inproj(x2, a_norm[l], w_main[l], w_f[l], b_f[l],
                                                  batch, seq)
        fqt, ka = _cumsum(lf, batch, seq)
        a = _attention(qt, k, ka, vt, fqt, batch, seq)
        x2 = _mix(a, u, cv, cc, cb, g, x2, b_g[l], w_pa[l], w_pool[l], p_scale[l], w_pp[l],
                  conv_w[l], w_pc[l], w_o[l], seq)
        x2 = _ffn(x2, f_norm[l], w_g[l], w_u[l], w_d[l], fn, final=(l == depth - 1))
    return x2.reshape(batch, seq, d)
```

```python
import functools
import math

import jax
import jax.numpy as jnp
from jax import lax
from jax.experimental import pallas as pl
from jax.experimental.pallas import tpu as pltpu

D_MODEL = 1024
HEADS = 8
HEAD_DIM = 64
BRANCH_W = 512
MAIN_COLS = 13 * BRANCH_W
POOL_WINDOWS = (2, 4, 8, 16)
POOL_GROUP_DIM = 128
CONV_K = 3
FFN_HIDDEN = 2816
FFN_CHUNKS = ((0, 1536), (1536, 2816))
RMS_EPS = 1e-6
NEG_INF = -1e30
LOG2E = math.log2(math.e)
LANES = 128
SUBLANES = 8
HALO = 16
HEADS_PER_SLAB = LANES // HEAD_DIM
F_TERMS = 3

BF16 = jnp.bfloat16
F32 = jnp.float32

TM_PROJ = 512
TM_MIX = 512
TM_FFN = 512
TQ = 512
TK = 512
CUMSUM_CHUNK = 256
QK_LOOKAHEAD = 4
ACC_ROWS = HEAD_DIM + SUBLANES
ACC_ROWS_PADDED = HEAD_DIM + 16
VMEM_LIMIT = 56 * 1024 * 1024


def _rms(x, w):
    return x * lax.rsqrt(jnp.mean(x * x, axis=-1, keepdims=True) + RMS_EPS) * w


def _const_spec(shape):
    nd = len(shape)
    return pl.BlockSpec(shape, lambda *_: (0,) * nd, pipeline_mode=pl.Buffered(1))


def _layer_spec(stacked, layer):
    rest = stacked.shape[1:]
    return pl.BlockSpec((None,) + rest, lambda *_: (layer,) + (0,) * len(rest),
                        pipeline_mode=pl.Buffered(1))


def _params(sem):
    return pltpu.CompilerParams(dimension_semantics=sem, vmem_limit_bytes=VMEM_LIMIT)


def _inproj_kernel(x_ref, nw_ref, w_ref, bf_ref, bg_ref,
                   qt_ref, k_ref, vt_ref, u_ref, z_ref, cb_ref, gate_ref, lf_ref):
    h = _rms(x_ref[...], nw_ref[...]).astype(BF16)

    def proj(n):
        return jnp.dot(h, w_ref[:, n * BRANCH_W:(n + 1) * BRANCH_W], preferred_element_type=F32)

    qt = proj(0).T.astype(BF16)
    for c in range(qt_ref.shape[1]):
        qt_ref[0, c] = qt[:, c * TQ:(c + 1) * TQ]
    k_ref[...] = proj(1).astype(BF16)
    vt = proj(2).T.astype(BF16)
    for c in range(vt_ref.shape[1]):
        vt_ref[0, c] = vt[:, c * TK:(c + 1) * TK]
    u_ref[...] = proj(3).astype(BF16)
    cb_ref[...] = proj(5).astype(BF16)
    z_ref[...] = (proj(4) * proj(6)).astype(BF16)
    for n in range(3 * D_MODEL // BRANCH_W):
        cs = slice(n * BRANCH_W, (n + 1) * BRANCH_W)
        gate_ref[:, cs] = (1.0 + jnp.tanh(proj(7 + n) + bg_ref[:, cs])).astype(BF16)
    f = jnp.dot(h, w_ref[:, MAIN_COLS:], preferred_element_type=F32) + bf_ref[...]
    lf_ref[...] = jnp.minimum(f, 0.0) - jnp.log1p(jnp.exp(-jnp.abs(f)))


def _inproj(x2, layer, nw, w_main, b_f, b_g, batch, seq):
    n = x2.shape[0]
    tm = TM_PROJ
    tiles_per_seq = seq // tm
    row = lambda c: pl.BlockSpec((tm, c), lambda i: (i, 0))
    assert TQ == TK
    vt_spec = pl.BlockSpec((1, tm // TK, BRANCH_W, TK),
                           lambda i: (i // tiles_per_seq, i % tiles_per_seq, 0, 0))
    act = jax.ShapeDtypeStruct((n, BRANCH_W), BF16)
    act_t = jax.ShapeDtypeStruct((batch, seq // TK, BRANCH_W, TK), BF16)
    outs = [act_t, act, act_t, act, act, act,
            jax.ShapeDtypeStruct((n, 3 * D_MODEL), BF16), jax.ShapeDtypeStruct((n, LANES), F32)]
    return pl.pallas_call(
        _inproj_kernel,
        grid=(n // tm,),
        in_specs=[row(D_MODEL)] + [_layer_spec(t, layer) for t in (nw, w_main, b_f, b_g)],
        out_specs=[vt_spec, row(BRANCH_W), vt_spec] + [row(BRANCH_W)] * 3
                  + [row(3 * D_MODEL), row(LANES)],
        out_shape=outs,
        compiler_params=_params(("parallel",)),
        name="inproj",
    )(x2, nw, w_main, b_f, b_g)


def _cumsum_kernel(lf_ref, fqt_ref, ka_ref):
    c = CUMSUM_CHUNK
    r = lax.broadcasted_iota(jnp.int32, (c, c), 0)
    s = lax.broadcasted_iota(jnp.int32, (c, c), 1)
    tri = (s <= r).astype(BF16)

    src = lax.broadcasted_iota(jnp.int32, (F_TERMS * LANES, BRANCH_W), 0)
    dst = lax.broadcasted_iota(jnp.int32, (F_TERMS * LANES, BRANCH_W), 1)
    head, term = src % LANES, src // LANES
    want = (head // HEADS_PER_SLAB) * LANES + F_TERMS * (head % HEADS_PER_SLAB) + term
    place = ((head < HEADS) & (dst == want)).astype(BF16)

    def split(x):
        hi = x.astype(BF16)
        r1 = x - hi.astype(F32)
        mid = r1.astype(BF16)
        lo = (r1 - mid.astype(F32)).astype(BF16)
        return jnp.concatenate([hi, mid, lo], axis=1)

    carry = jnp.zeros((1, LANES), F32)
    for n in range(lf_ref.shape[0] // c):
        rows = slice(n * c, (n + 1) * c)
        parts = jnp.dot(tri, split(lf_ref[rows, :]), preferred_element_type=F32)
        cum = parts[:, :LANES] + parts[:, LANES:2 * LANES] + parts[:, 2 * LANES:] + carry
        carry = cum[c - 1:c, :]
        f2 = cum * LOG2E
        fqt_ref[0, :, rows] = f2.T[:HEADS, :]
        ka_ref[rows, :] = jnp.dot(split(-f2), place, preferred_element_type=F32).astype(BF16)


def _cumsum(lf, batch, seq):
    return pl.pallas_call(
        _cumsum_kernel,
        grid=(batch,),
        in_specs=[pl.BlockSpec((seq, LANES), lambda b: (b, 0))],
        out_specs=[pl.BlockSpec((1, HEADS, seq), lambda b: (b, 0, 0)),
                   pl.BlockSpec((seq, BRANCH_W), lambda b: (b, 0))],
        out_shape=[jax.ShapeDtypeStruct((batch, HEADS, seq), F32),
                   jax.ShapeDtypeStruct((batch * seq, BRANCH_W), BF16)],
        compiler_params=_params(("parallel",)),
        name="forget_cumsum",
    )(lf)


def _sublane_allreduce(x, op):
    shift = SUBLANES // 2
    while shift:
        x = op(x, pltpu.roll(x, shift, axis=0))
        shift //= 2
    return x


def _attn_kernel(qt_ref, k_ref, ka_ref, vt_ref, fq_ref, o_ref,
                 qa_ref, m_ref, acc_ref, *, tq, tk, nq):
    groups = tk // SUBLANES
    chan = lax.broadcasted_iota(jnp.int32, (LANES, 1), 0)
    key_pos = lax.broadcasted_iota(jnp.int32, (tk, tq), 0)
    query_pos = lax.broadcasted_iota(jnp.int32, (tk, tq), 1)
    causal = key_pos <= query_pos
    ones_rows = jnp.ones((ACC_ROWS_PADDED - HEAD_DIM, tk), BF16)

    def begin_block(qi):
        for h in range(HEADS):
            slab, e = divmod(h, HEADS_PER_SLAB)
            q2 = qt_ref[0, qi, slab * LANES:(slab + 1) * LANES, :]
            qm = jnp.where(chan // HEAD_DIM == e, q2, jnp.zeros_like(q2))
            ones = ((chan >= F_TERMS * e) & (chan < F_TERMS * (e + 1))).astype(BF16)
            qa_ref[qi % 2, h] = jnp.concatenate([qm, jnp.broadcast_to(ones, (LANES, tq))], axis=0)
        m_ref[qi % 2] = jnp.full(m_ref.shape[1:], NEG_INF, F32)
        acc_ref[qi % 2] = jnp.zeros(acc_ref.shape[1:], F32)

    def logits_t(qi, j, h):
        rows = slice(j * tk, (j + 1) * tk)
        cols = slice(h // HEADS_PER_SLAB * LANES, (h // HEADS_PER_SLAB + 1) * LANES)
        lhs = jnp.concatenate([k_ref[rows, cols], ka_ref[rows, cols]], axis=1)
        return jnp.dot(lhs, qa_ref[qi % 2, h], preferred_element_type=F32)

    def softmax_pv(qi, j, h, st):
        if j == qi:
            st = jnp.where(causal, st, NEG_INF)
        st3 = st.reshape(groups, SUBLANES, tq)
        stat = slice(h * SUBLANES, (h + 1) * SUBLANES)
        fq = jnp.broadcast_to(fq_ref[0, h:h + 1, qi * tq:(qi + 1) * tq], (SUBLANES, tq))
        m_old = m_ref[qi % 2, stat, :]
        m_new = jnp.maximum(m_old, _sublane_allreduce(jnp.max(st3, axis=0), jnp.maximum) + fq)
        p = jnp.exp2(st3 - (m_new - fq)[None]).reshape(tk, tq).astype(BF16)
        alpha = jnp.exp2(m_old - m_new)
        m_ref[qi % 2, stat, :] = m_new
        v_ones = jnp.concatenate([vt_ref[0, j, h * HEAD_DIM:(h + 1) * HEAD_DIM, :], ones_rows], axis=0)
        pv = jnp.dot(v_ones, p, preferred_element_type=F32)[:ACC_ROWS]
        acc3 = acc_ref[qi % 2, h].reshape(ACC_ROWS // SUBLANES, SUBLANES, tq)
        acc_ref[qi % 2, h] = (acc3 * alpha[None]).reshape(ACC_ROWS, tq) + pv

    def end_block(qi):
        outs = []
        for h in range(HEADS):
            acc3 = acc_ref[qi % 2, h].reshape(ACC_ROWS // SUBLANES, SUBLANES, tq)
            outs.append((acc3[:HEAD_DIM // SUBLANES] / acc3[HEAD_DIM // SUBLANES][None])
                        .reshape(HEAD_DIM, tq))
        o_ref[qi * tq:(qi + 1) * tq, :] = jnp.concatenate(outs, axis=0).T.astype(BF16)

    items = [(qi, j, h) for qi in range(nq) for j in range(qi + 1) for h in range(HEADS)]
    begun, pending = set(), []

    def issue(n):
        qi = items[n][0]
        if qi not in begun:
            begun.add(qi)
            begin_block(qi)
        pending.append(logits_t(*items[n]))

    for n in range(min(QK_LOOKAHEAD, len(items))):
        issue(n)
    for n, (qi, j, h) in enumerate(items):
        if n + QK_LOOKAHEAD < len(items):
            issue(n + QK_LOOKAHEAD)
        softmax_pv(qi, j, h, pending.pop(0))
        if j == qi and h == HEADS - 1:
            end_block(qi)


def _attention(qt, k, ka, vt, fqt, batch, seq):
    tq, tk = TQ, TK
    assert tq == tk
    nq = seq // tq
    w = HEADS * HEAD_DIM
    return pl.pallas_call(
        functools.partial(_attn_kernel, tq=tq, tk=tk, nq=nq),
        grid=(batch,),
        in_specs=[pl.BlockSpec((1, nq, w, tq), lambda b: (b, 0, 0, 0)),
                  pl.BlockSpec((seq, w), lambda b: (b, 0)),
                  pl.BlockSpec((seq, w), lambda b: (b, 0)),
                  pl.BlockSpec((1, seq // tk, w, tk), lambda b: (b, 0, 0, 0)),
                  pl.BlockSpec((1, HEADS, seq), lambda b: (b, 0, 0))],
        out_specs=pl.BlockSpec((seq, w), lambda b: (b, 0)),
        out_shape=jax.ShapeDtypeStruct(k.shape, BF16),
        scratch_shapes=[pltpu.VMEM((2, HEADS, 2 * LANES, tq), BF16),
                        pltpu.VMEM((2, HEADS * SUBLANES, tq), F32),
                        pltpu.VMEM((2, HEADS, ACC_ROWS, tq), F32)],
        compiler_params=_params(("parallel",)),
        name="fox_attention",
    )(qt, k, ka, vt, fqt)


def _mix_kernel(a_ref, u_ref, up_ref, z_ref, zp_ref, cb_ref, gate_ref, x_ref,
                wpa_ref, pw_ref, ps_ref, wpp_ref, cw_ref, wpc_ref, wo_ref,
                o_ref, ue_ref, ze_ref, *, tm, tiles_per_seq):
    i = pl.program_id(0) % tiles_per_seq
    has_prev = i != 0
    pad, first = SUBLANES, SUBLANES + HALO

    u = u_ref[...].astype(F32)
    ue_ref[0:pad, :] = jnp.zeros((pad, BRANCH_W), F32)
    ue_ref[pad:first, :] = jnp.where(has_prev, up_ref[...].astype(F32), 0.0)
    ue_ref[first:, :] = u
    ze_ref[0:HALO, :] = jnp.where(has_prev, zp_ref[...].astype(F32), 0.0)
    ze_ref[HALO:, :] = z_ref[...].astype(F32)

    y_a = jnp.dot(a_ref[...], wpa_ref[...], preferred_element_type=F32)

    for level in range(len(POOL_WINDOWS) - 1):
        lag = 2 ** level
        cs = slice(level * POOL_GROUP_DIM, BRANCH_W)
        ue_ref[pad:, cs] = ue_ref[pad:, cs] + ue_ref[pad - lag:pad - lag + tm + HALO, cs]
    pos = i * tm + lax.broadcasted_iota(jnp.int32, (tm, 1), 0)
    ys = []
    for gi, w in enumerate(POOL_WINDOWS):
        cs = slice(gi * POOL_GROUP_DIM, (gi + 1) * POOL_GROUP_DIM)
        win = ue_ref[first:, cs]
        if gi == len(POOL_WINDOWS) - 1:
            win = win + ue_ref[first - w // 2:first - w // 2 + tm, cs]
        n_avail = jnp.minimum(pos + 1, w).astype(F32)
        d = win / n_avail - u[:, cs]
        y = jnp.dot(d.astype(BF16), pw_ref[gi], preferred_element_type=F32)
        ys.append((y * ps_ref[:, cs]).astype(BF16))
    y_b = jnp.dot(jnp.concatenate(ys, axis=1), wpp_ref[...], preferred_element_type=F32)

    conv = cw_ref[CONV_K - 1:CONV_K, :] * ze_ref[HALO:, :]
    for lag in range(1, CONV_K):
        conv = conv + cw_ref[CONV_K - 1 - lag:CONV_K - lag, :] * ze_ref[HALO - lag:HALO - lag + tm, :]
    y_c = jnp.dot((cb_ref[...].astype(F32) * conv).astype(BF16), wpc_ref[...],
                  preferred_element_type=F32)

    mixed = None
    for n, y in enumerate((y_a, y_b, y_c)):
        gate2 = gate_ref[:, n * D_MODEL:(n + 1) * D_MODEL].astype(F32)
        mixed = gate2 * y if mixed is None else mixed + gate2 * y
    o_ref[...] = x_ref[...] + jnp.dot(mixed.astype(BF16), wo_ref[...], preferred_element_type=F32)


def _mix(a, u, z, cb, gate, x2, layer, wpa, pw, ps, wpp, cw, wpc, wo, seq):
    n = x2.shape[0]
    tm = TM_MIX
    tiles_per_seq = seq // tm
    row = lambda c: pl.BlockSpec((tm, c), lambda i: (i, 0))
    prev = pl.BlockSpec((HALO, BRANCH_W), lambda i: (jnp.maximum(i * (tm // HALO) - 1, 0), 0))
    return pl.pallas_call(
        functools.partial(_mix_kernel, tm=tm, tiles_per_seq=tiles_per_seq),
        grid=(n // tm,),
        in_specs=[row(BRANCH_W), row(BRANCH_W), prev, row(BRANCH_W), prev,
                  row(BRANCH_W), row(3 * D_MODEL), row(D_MODEL)]
                 + [_layer_spec(t, layer) for t in (wpa, pw, ps, wpp, cw, wpc, wo)],
        out_specs=row(D_MODEL),
        out_shape=jax.ShapeDtypeStruct(x2.shape, F32),
        scratch_shapes=[pltpu.VMEM((SUBLANES + HALO + tm, BRANCH_W), F32),
                        pltpu.VMEM((HALO + tm, BRANCH_W), F32)],
        compiler_params=_params(("parallel",)),
        name="branch_mix",
    )(a, u, u, z, z, cb, gate, x2, wpa, pw, ps, wpp, cw, wpc, wo)


def _ffn_kernel(x_ref, nw_ref, wgu_ref, wd_ref, fn_ref, o_ref, *, final):
    x = x_ref[...]
    h = _rms(x, nw_ref[...]).astype(BF16)
    acc = x
    for lo, hi in FFN_CHUNKS:
        a = jnp.dot(h, wgu_ref[:, lo:hi], preferred_element_type=F32)
        b = jnp.dot(h, wgu_ref[:, FFN_HIDDEN + lo:FFN_HIDDEN + hi], preferred_element_type=F32)
        act = (a * jax.nn.sigmoid(a) * b).astype(BF16)
        acc = acc + jnp.dot(act, wd_ref[lo:hi, :], preferred_element_type=F32)
    if final:
        acc = _rms(acc, fn_ref[...])
    o_ref[...] = acc


def _ffn(x2, layer, nw, wgu, wd, fn, final):
    n = x2.shape[0]
    tm = TM_FFN
    row = pl.BlockSpec((tm, D_MODEL), lambda i: (i, 0))
    return pl.pallas_call(
        functools.partial(_ffn_kernel, final=final),
        grid=(n // tm,),
        in_specs=[row] + [_layer_spec(t, layer) for t in (nw, wgu, wd)] + [_const_spec(fn.shape)],
        out_specs=row,
        out_shape=jax.ShapeDtypeStruct(x2.shape, F32),
        compiler_params=_params(("parallel",)),
        name="swiglu_ffn",
    )(x2, nw, wgu, wd, fn)


def kernel(x, attn_norm, w_in, b_forget, b_gate, w_proj_attn, pool_w, pool_scale, w_proj_pool,
           conv_w, w_proj_conv, w_out, ffn_norm, w_gate_up, w_down, final_norm):
    batch, seq, d = x.shape
    depth = w_in.shape[0]
    x2 = x.reshape(batch * seq, d)
    qkv_w = 3 * BRANCH_W
    gate_lo = qkv_w + HEADS + 4 * BRANCH_W
    w_main = jnp.concatenate([w_in[:, :, :BRANCH_W] * (LOG2E * HEAD_DIM ** -0.5),
                              w_in[:, :, BRANCH_W:qkv_w],
                              w_in[:, :, qkv_w + HEADS:gate_lo],
                              w_in[:, :, gate_lo:] * 0.5,
                              w_in[:, :, qkv_w:qkv_w + HEADS],
                              jnp.zeros((depth, d, LANES - HEADS), F32)], axis=2).astype(BF16)
    b_f = jnp.pad(b_forget, ((0, 0), (0, LANES - HEADS))).reshape(depth, 1, LANES)
    b_g = (b_gate * 0.5).reshape(depth, 1, -1)
    w_pa, w_pp, w_pc = (w.astype(BF16) for w in (w_proj_attn, w_proj_pool, w_proj_conv))
    w_o = (w_out * 0.5).astype(BF16)
    w_pool = pool_w.astype(BF16)
    w_gu = w_gate_up.astype(BF16)
    w_d = w_down.astype(BF16)
    a_norm, f_norm = attn_norm.reshape(depth, 1, d), ffn_norm.reshape(depth, 1, d)
    p_scale = pool_scale.reshape(depth, 1, -1)
    fn = final_norm.reshape(1, d)
    for l in range(depth):
        qt, k, vt, u, z, cb, gate, lf = _inproj(x2, l, a_norm, w_main, b_f, b_g, batch, seq)
        fqt, ka = _cumsum(lf, batch, seq)
        a = _attention(qt, k, ka, vt, fqt, batch, seq)
        x2 = _mix(a, u, z, cb, gate, x2, l, w_pa, w_pool, p_scale, w_pp, conv_w, w_pc, w_o, seq)
        x2 = _ffn(x2, l, f_norm, w_gu, w_d, fn, final=(l == depth - 1))
    return x2.reshape(batch, seq, d)
```

```python
import functools
import math

import jax
import jax.numpy as jnp
from jax import lax
from jax.experimental import pallas as pl
from jax.experimental.pallas import tpu as pltpu

D_MODEL = 1024
HEADS = 8
HEAD_DIM = 64
BRANCH_W = 512
MAIN_COLS = 13 * BRANCH_W
POOL_WINDOWS = (2, 4, 8, 16)
POOL_GROUP_DIM = 128
CONV_K = 3
FFN_HIDDEN = 2816
FFN_CHUNKS = ((0, 1536), (1536, 2816))
RMS_EPS = 1e-6
NEG_INF = -1e30
LOG2E = math.log2(math.e)
LANES = 128
SUBLANES = 8
HALO = 16
HEADS_PER_SLAB = LANES // HEAD_DIM
F_TERMS = 3

BF16 = jnp.bfloat16
F32 = jnp.float32

TM_PROJ = 512
TM_MIX = 512
TM_FFN = 512
TQ = 512
TK = 512
CUMSUM_CHUNK = 256
QK_LOOKAHEAD = 4
ACC_ROWS = HEAD_DIM + SUBLANES
ACC_ROWS_PADDED = HEAD_DIM + 16
VMEM_LIMIT = 56 * 1024 * 1024


def _rms(x, w):
    return x * lax.rsqrt(jnp.mean(x * x, axis=-1, keepdims=True) + RMS_EPS) * w


def _const_spec(shape):
    nd = len(shape)
    return pl.BlockSpec(shape, lambda *_: (0,) * nd, pipeline_mode=pl.Buffered(1))


def _layer_spec(stacked, layer):
    rest = stacked.shape[1:]
    return pl.BlockSpec((None,) + rest, lambda *_: (layer,) + (0,) * len(rest),
                        pipeline_mode=pl.Buffered(1))


def _params(sem):
    return pltpu.CompilerParams(dimension_semantics=sem, vmem_limit_bytes=VMEM_LIMIT)


def _prep_w_in_kernel(w_ref, o_ref):
    w = w_ref[...]
    qkv_w = 3 * BRANCH_W
    gate_lo = qkv_w + HEADS + 4 * BRANCH_W
    o_ref[:, :BRANCH_W] = (w[:, :BRANCH_W] * (LOG2E * HEAD_DIM ** -0.5)).astype(BF16)
    o_ref[:, BRANCH_W:qkv_w] = w[:, BRANCH_W:qkv_w].astype(BF16)
    o_ref[:, qkv_w:MAIN_COLS - 3 * D_MODEL] = w[:, qkv_w + HEADS:gate_lo].astype(BF16)
    o_ref[:, MAIN_COLS - 3 * D_MODEL:MAIN_COLS] = (w[:, gate_lo:] * 0.5).astype(BF16)
    f = w[:, qkv_w:qkv_w + HEADS]
    o_ref[:, MAIN_COLS:] = jnp.concatenate(
        [f, jnp.zeros((f.shape[0], LANES - HEADS), F32)], axis=1).astype(BF16)


def _prep_w_in(w_in):
    depth, k, n = w_in.shape
    rows = 256
    return pl.pallas_call(
        _prep_w_in_kernel,
        grid=(depth, k // rows),
        in_specs=[pl.BlockSpec((None, rows, n), lambda l, i: (l, i, 0))],
        out_specs=pl.BlockSpec((None, rows, MAIN_COLS + LANES), lambda l, i: (l, i, 0)),
        out_shape=jax.ShapeDtypeStruct((depth, k, MAIN_COLS + LANES), BF16),
        compiler_params=_params(("parallel", "parallel")),
        name="prep_w_in",
    )(w_in)


def _inproj_kernel(x_ref, nw_ref, w_ref, bf_ref, bg_ref,
                   qt_ref, k_ref, vt_ref, u_ref, z_ref, cb_ref, gate_ref, lf_ref):
    h = _rms(x_ref[...], nw_ref[...]).astype(BF16)

    def proj(n):
        return jnp.dot(h, w_ref[:, n * BRANCH_W:(n + 1) * BRANCH_W], preferred_element_type=F32)

    qt = proj(0).T.astype(BF16)
    for c in range(qt_ref.shape[1]):
        qt_ref[0, c] = qt[:, c * TQ:(c + 1) * TQ]
    k_ref[...] = proj(1).astype(BF16)
    vt = proj(2).T.astype(BF16)
    for c in range(vt_ref.shape[1]):
        vt_ref[0, c] = vt[:, c * TK:(c + 1) * TK]
    u_ref[...] = proj(3).astype(BF16)
    cb_ref[...] = proj(5).astype(BF16)
    z_ref[...] = (proj(4) * proj(6)).astype(BF16)
    for n in range(3 * D_MODEL // BRANCH_W):
        cs = slice(n * BRANCH_W, (n + 1) * BRANCH_W)
        gate_ref[:, cs] = (1.0 + jnp.tanh(proj(7 + n) + bg_ref[:, cs])).astype(BF16)
    f = jnp.dot(h, w_ref[:, MAIN_COLS:], preferred_element_type=F32) + bf_ref[...]
    lf_ref[...] = jnp.minimum(f, 0.0) - jnp.log1p(jnp.exp(-jnp.abs(f)))


def _inproj(x2, layer, nw, w_main, b_f, b_g, batch, seq):
    n = x2.shape[0]
    tm = TM_PROJ
    tiles_per_seq = seq // tm
    row = lambda c: pl.BlockSpec((tm, c), lambda i: (i, 0))
    assert TQ == TK
    vt_spec = pl.BlockSpec((1, tm // TK, BRANCH_W, TK),
                           lambda i: (i // tiles_per_seq, i % tiles_per_seq, 0, 0))
    act = jax.ShapeDtypeStruct((n, BRANCH_W), BF16)
    act_t = jax.ShapeDtypeStruct((batch, seq // TK, BRANCH_W, TK), BF16)
    outs = [act_t, act, act_t, act, act, act,
            jax.ShapeDtypeStruct((n, 3 * D_MODEL), BF16), jax.ShapeDtypeStruct((n, LANES), F32)]
    return pl.pallas_call(
        _inproj_kernel,
        grid=(n // tm,),
        in_specs=[row(D_MODEL)] + [_layer_spec(t, layer) for t in (nw, w_main, b_f, b_g)],
        out_specs=[vt_spec, row(BRANCH_W), vt_spec] + [row(BRANCH_W)] * 3
                  + [row(3 * D_MODEL), row(LANES)],
        out_shape=outs,
        compiler_params=_params(("parallel",)),
        name="inproj",
    )(x2, nw, w_main, b_f, b_g)


def _cumsum_kernel(lf_ref, fqt_ref, ka_ref):
    c = CUMSUM_CHUNK
    r = lax.broadcasted_iota(jnp.int32, (c, c), 0)
    s = lax.broadcasted_iota(jnp.int32, (c, c), 1)
    tri = (s <= r).astype(BF16)

    src = lax.broadcasted_iota(jnp.int32, (F_TERMS * LANES, BRANCH_W), 0)
    dst = lax.broadcasted_iota(jnp.int32, (F_TERMS * LANES, BRANCH_W), 1)
    head, term = src % LANES, src // LANES
    want = (head // HEADS_PER_SLAB) * LANES + F_TERMS * (head % HEADS_PER_SLAB) + term
    place = ((head < HEADS) & (dst == want)).astype(BF16)

    def split(x):
        hi = x.astype(BF16)
        r1 = x - hi.astype(F32)
        mid = r1.astype(BF16)
        lo = (r1 - mid.astype(F32)).astype(BF16)
        return jnp.concatenate([hi, mid, lo], axis=1)

    carry = jnp.zeros((1, LANES), F32)
    for n in range(lf_ref.shape[0] // c):
        rows = slice(n * c, (n + 1) * c)
        parts = jnp.dot(tri, split(lf_ref[rows, :]), preferred_element_type=F32)
        cum = parts[:, :LANES] + parts[:, LANES:2 * LANES] + parts[:, 2 * LANES:] + carry
        carry = cum[c - 1:c, :]
        f2 = cum * LOG2E
        fqt_ref[0, :, rows] = f2.T[:HEADS, :]
        ka_ref[rows, :] = jnp.dot(split(-f2), place, preferred_element_type=F32).astype(BF16)


def _cumsum(lf, batch, seq):
    return pl.pallas_call(
        _cumsum_kernel,
        grid=(batch,),
        in_specs=[pl.BlockSpec((seq, LANES), lambda b: (b, 0))],
        out_specs=[pl.BlockSpec((1, HEADS, seq), lambda b: (b, 0, 0)),
                   pl.BlockSpec((seq, BRANCH_W), lambda b: (b, 0))],
        out_shape=[jax.ShapeDtypeStruct((batch, HEADS, seq), F32),
                   jax.ShapeDtypeStruct((batch * seq, BRANCH_W), BF16)],
        compiler_params=_params(("parallel",)),
        name="forget_cumsum",
    )(lf)


def _sublane_allreduce(x, op):
    shift = SUBLANES // 2
    while shift:
        x = op(x, pltpu.roll(x, shift, axis=0))
        shift //= 2
    return x


def _attn_kernel(qt_ref, k_ref, ka_ref, vt_ref, fq_ref, o_ref,
                 qa_ref, m_ref, acc_ref, *, tq, tk, nq):
    chan = lax.broadcasted_iota(jnp.int32, (LANES, 1), 0)
    ones_rows = jnp.ones((ACC_ROWS_PADDED - HEAD_DIM, tk), BF16)

    def begin_block(qi):
        for h in range(HEADS):
            slab, e = divmod(h, HEADS_PER_SLAB)
            q2 = qt_ref[0, qi, slab * LANES:(slab + 1) * LANES, :]
            qm = jnp.where(chan // HEAD_DIM == e, q2, jnp.zeros_like(q2))
            ones = ((chan >= F_TERMS * e) & (chan < F_TERMS * (e + 1))).astype(BF16)
            qa_ref[qi % 2, h] = jnp.concatenate([qm, jnp.broadcast_to(ones, (LANES, tq))], axis=0)
        m_ref[qi % 2] = jnp.full(m_ref.shape[1:], NEG_INF, F32)
        acc_ref[qi % 2] = jnp.zeros(acc_ref.shape[1:], F32)

    def logits_t(qi, j, h, nk, c0, nc):
        rows = slice(j * tk, j * tk + nk)
        cols = slice(h // HEADS_PER_SLAB * LANES, (h // HEADS_PER_SLAB + 1) * LANES)
        lhs = jnp.concatenate([k_ref[rows, cols], ka_ref[rows, cols]], axis=1)
        return jnp.dot(lhs, qa_ref[qi % 2, h, :, c0:c0 + nc],
                       preferred_element_type=F32)

    def softmax_pv(qi, j, h, nk, c0, nc, st):
        if j == qi:
            key_pos = lax.broadcasted_iota(jnp.int32, (nk, nc), 0)
            query_pos = c0 + lax.broadcasted_iota(jnp.int32, (nk, nc), 1)
            st = jnp.where(key_pos <= query_pos, st, NEG_INF)
        st3 = st.reshape(nk // SUBLANES, SUBLANES, nc)
        stat = slice(h * SUBLANES, (h + 1) * SUBLANES)
        qc = slice(c0, c0 + nc)
        fq = jnp.broadcast_to(fq_ref[0, h:h + 1, qi * tq + c0:qi * tq + c0 + nc], (SUBLANES, nc))
        m_old = m_ref[qi % 2, stat, qc]
        m_new = jnp.maximum(m_old, _sublane_allreduce(jnp.max(st3, axis=0), jnp.maximum) + fq)
        p = jnp.exp2(st3 - (m_new - fq)[None]).reshape(nk, nc).astype(BF16)
        alpha = jnp.exp2(m_old - m_new)
        m_ref[qi % 2, stat, qc] = m_new
        v_ones = jnp.concatenate([vt_ref[0, j, h * HEAD_DIM:(h + 1) * HEAD_DIM, :nk],
                                  ones_rows[:, :nk]], axis=0)
        pv = jnp.dot(v_ones, p, preferred_element_type=F32)[:ACC_ROWS]
        acc3 = acc_ref[qi % 2, h, :, qc].reshape(ACC_ROWS // SUBLANES, SUBLANES, nc)
        acc_ref[qi % 2, h, :, qc] = (acc3 * alpha[None]).reshape(ACC_ROWS, nc) + pv

    def end_block(qi):
        outs = []
        for h in range(HEADS):
            acc3 = acc_ref[qi % 2, h].reshape(ACC_ROWS // SUBLANES, SUBLANES, tq)
            outs.append((acc3[:HEAD_DIM // SUBLANES] / acc3[HEAD_DIM // SUBLANES][None])
                        .reshape(HEAD_DIM, tq))
        o_ref[qi * tq:(qi + 1) * tq, :] = jnp.concatenate(outs, axis=0).T.astype(BF16)

    items = []
    for qi in range(nq):
        for j in range(qi + 1):
            tiles = [(tk, 0, tq)] if j < qi else [(tk // 2, 0, tq // 2), (tk, tq // 2, tq // 2)]
            items += [(qi, j, h) + tile for h in range(HEADS) for tile in tiles]
    begun, pending = set(), []

    def issue(n):
        qi = items[n][0]
        if qi not in begun:
            begun.add(qi)
            begin_block(qi)
        pending.append(logits_t(*items[n]))

    for n in range(min(QK_LOOKAHEAD, len(items))):
        issue(n)
    for n, item in enumerate(items):
        if n + QK_LOOKAHEAD < len(items):
            issue(n + QK_LOOKAHEAD)
        softmax_pv(*item, pending.pop(0))
        if n + 1 == len(items) or items[n + 1][0] != item[0]:
            end_block(item[0])


def _attention(qt, k, ka, vt, fqt, batch, seq):
    tq, tk = TQ, TK
    assert tq == tk
    nq = seq // tq
    w = HEADS * HEAD_DIM
    return pl.pallas_call(
        functools.partial(_attn_kernel, tq=tq, tk=tk, nq=nq),
        grid=(batch,),
        in_specs=[pl.BlockSpec((1, nq, w, tq), lambda b: (b, 0, 0, 0)),
                  pl.BlockSpec((seq, w), lambda b: (b, 0)),
                  pl.BlockSpec((seq, w), lambda b: (b, 0)),
                  pl.BlockSpec((1, seq // tk, w, tk), lambda b: (b, 0, 0, 0)),
                  pl.BlockSpec((1, HEADS, seq), lambda b: (b, 0, 0))],
        out_specs=pl.BlockSpec((seq, w), lambda b: (b, 0)),
        out_shape=jax.ShapeDtypeStruct(k.shape, BF16),
        scratch_shapes=[pltpu.VMEM((2, HEADS, 2 * LANES, tq), BF16),
                        pltpu.VMEM((2, HEADS * SUBLANES, tq), F32),
                        pltpu.VMEM((2, HEADS, ACC_ROWS, tq), F32)],
        compiler_params=_params(("parallel",)),
        name="fox_attention",
    )(qt, k, ka, vt, fqt)


def _mix_kernel(a_ref, u_ref, up_ref, z_ref, zp_ref, cb_ref, gate_ref, x_ref,
                wpa_ref, pw_ref, ps_ref, wpp_ref, cw_ref, wpc_ref, wo_ref,
                o_ref, ue_ref, ze_ref, *, tm, tiles_per_seq):
    i = pl.program_id(0) % tiles_per_seq
    has_prev = i != 0
    pad, first = SUBLANES, SUBLANES + HALO

    u = u_ref[...].astype(F32)
    ue_ref[0:pad, :] = jnp.zeros((pad, BRANCH_W), F32)
    ue_ref[pad:first, :] = jnp.where(has_prev, up_ref[...].astype(F32), 0.0)
    ue_ref[first:, :] = u
    ze_ref[0:HALO, :] = jnp.where(has_prev, zp_ref[...].astype(F32), 0.0)
    ze_ref[HALO:, :] = z_ref[...].astype(F32)

    y_a = jnp.dot(a_ref[...], wpa_ref[...], preferred_element_type=F32)

    for level in range(len(POOL_WINDOWS) - 1):
        lag = 2 ** level
        cs = slice(level * POOL_GROUP_DIM, BRANCH_W)
        ue_ref[pad:, cs] = ue_ref[pad:, cs] + ue_ref[pad - lag:pad - lag + tm + HALO, cs]
    pos = i * tm + lax.broadcasted_iota(jnp.int32, (tm, 1), 0)
    ys = []
    for gi, w in enumerate(POOL_WINDOWS):
        cs = slice(gi * POOL_GROUP_DIM, (gi + 1) * POOL_GROUP_DIM)
        win = ue_ref[first:, cs]
        if gi == len(POOL_WINDOWS) - 1:
            win = win + ue_ref[first - w // 2:first - w // 2 + tm, cs]
        n_avail = jnp.minimum(pos + 1, w).astype(F32)
        d = win / n_avail - u[:, cs]
        y = jnp.dot(d.astype(BF16), pw_ref[gi], preferred_element_type=F32)
        ys.append((y * ps_ref[:, cs]).astype(BF16))
    y_b = jnp.dot(jnp.concatenate(ys, axis=1), wpp_ref[...], preferred_element_type=F32)

    conv = cw_ref[CONV_K - 1:CONV_K, :] * ze_ref[HALO:, :]
    for lag in range(1, CONV_K):
        conv = conv + cw_ref[CONV_K - 1 - lag:CONV_K - lag, :] * ze_ref[HALO - lag:HALO - lag + tm, :]
    y_c = jnp.dot((cb_ref[...].astype(F32) * conv).astype(BF16), wpc_ref[...],
                  preferred_element_type=F32)

    mixed = None
    for n, y in enumerate((y_a, y_b, y_c)):
        gate2 = gate_ref[:, n * D_MODEL:(n + 1) * D_MODEL].astype(F32)
        mixed = gate2 * y if mixed is None else mixed + gate2 * y
    o_ref[...] = x_ref[...] + jnp.dot(mixed.astype(BF16), wo_ref[...], preferred_element_type=F32)


def _mix(a, u, z, cb, gate, x2, layer, wpa, pw, ps, wpp, cw, wpc, wo, seq):
    n = x2.shape[0]
    tm = TM_MIX
    tiles_per_seq = seq // tm
    row = lambda c: pl.BlockSpec((tm, c), lambda i: (i, 0))
    prev = pl.BlockSpec((HALO, BRANCH_W), lambda i: (jnp.maximum(i * (tm // HALO) - 1, 0), 0))
    return pl.pallas_call(
        functools.partial(_mix_kernel, tm=tm, tiles_per_seq=tiles_per_seq),
        grid=(n // tm,),
        in_specs=[row(BRANCH_W), row(BRANCH_W), prev, row(BRANCH_W), prev,
                  row(BRANCH_W), row(3 * D_MODEL), row(D_MODEL)]
                 + [_layer_spec(t, layer) for t in (wpa, pw, ps, wpp, cw, wpc, wo)],
        out_specs=row(D_MODEL),
        out_shape=jax.ShapeDtypeStruct(x2.shape, F32),
        scratch_shapes=[pltpu.VMEM((SUBLANES + HALO + tm, BRANCH_W), F32),
                        pltpu.VMEM((HALO + tm, BRANCH_W), F32)],
        compiler_params=_params(("parallel",)),
        name="branch_mix",
    )(a, u, u, z, z, cb, gate, x2, wpa, pw, ps, wpp, cw, wpc, wo)


def _ffn_kernel(x_ref, nw_ref, wgu_ref, wd_ref, fn_ref, o_ref, *, final):
    x = x_ref[...]
    h = _rms(x, nw_ref[...]).astype(BF16)
    acc = x
    for lo, hi in FFN_CHUNKS:
        a = jnp.dot(h, wgu_ref[:, lo:hi], preferred_element_type=F32)
        b = jnp.dot(h, wgu_ref[:, FFN_HIDDEN + lo:FFN_HIDDEN + hi], preferred_element_type=F32)
        act = (a * jax.nn.sigmoid(a) * b).astype(BF16)
        acc = acc + jnp.dot(act, wd_ref[lo:hi, :], preferred_element_type=F32)
    if final:
        acc = _rms(acc, fn_ref[...])
    o_ref[...] = acc


def _ffn(x2, layer, nw, wgu, wd, fn, final):
    n = x2.shape[0]
    tm = TM_FFN
    row = pl.BlockSpec((tm, D_MODEL), lambda i: (i, 0))
    return pl.pallas_call(
        functools.partial(_ffn_kernel, final=final),
        grid=(n // tm,),
        in_specs=[row] + [_layer_spec(t, layer) for t in (nw, wgu, wd)] + [_const_spec(fn.shape)],
        out_specs=row,
        out_shape=jax.ShapeDtypeStruct(x2.shape, F32),
        compiler_params=_params(("parallel",)),
        name="swiglu_ffn",
    )(x2, nw, wgu, wd, fn)


def kernel(x, attn_norm, w_in, b_forget, b_gate, w_proj_attn, pool_w, pool_scale, w_proj_pool,
           conv_w, w_proj_conv, w_out, ffn_norm, w_gate_up, w_down, final_norm):
    batch, seq, d = x.shape
    depth = w_in.shape[0]
    x2 = x.reshape(batch * seq, d)
    w_main = _prep_w_in(w_in)
    b_f = jnp.pad(b_forget, ((0, 0), (0, LANES - HEADS))).reshape(depth, 1, LANES)
    b_g = (b_gate * 0.5).reshape(depth, 1, -1)
    w_pa, w_pp, w_pc = (w.astype(BF16) for w in (w_proj_attn, w_proj_pool, w_proj_conv))
    w_o = (w_out * 0.5).astype(BF16)
    w_pool = pool_w.astype(BF16)
    w_gu = w_gate_up.astype(BF16)
    w_d = w_down.astype(BF16)
    a_norm, f_norm = attn_norm.reshape(depth, 1, d), ffn_norm.reshape(depth, 1, d)
    p_scale = pool_scale.reshape(depth, 1, -1)
    fn = final_norm.reshape(1, d)
    for l in range(depth):
        qt, k, vt, u, z, cb, gate, lf = _inproj(x2, l, a_norm, w_main, b_f, b_g, batch, seq)
        fqt, ka = _cumsum(lf, batch, seq)
        a = _attention(qt, k, ka, vt, fqt, batch, seq)
        x2 = _mix(a, u, z, cb, gate, x2, l, w_pa, w_pool, p_scale, w_pp, conv_w, w_pc, w_o, seq)
        x2 = _ffn(x2, l, f_norm, w_gu, w_d, fn, final=(l == depth - 1))
    return x2.reshape(batch, seq, d)
```

```python
import functools
import math

import jax
import jax.numpy as jnp
from jax import lax
from jax.experimental import pallas as pl
from jax.experimental.pallas import tpu as pltpu

D_MODEL = 1024
HEADS = 8
HEAD_DIM = 64
BRANCH_W = 512
MAIN_COLS = 13 * BRANCH_W
POOL_WINDOWS = (2, 4, 8, 16)
POOL_GROUP_DIM = 128
CONV_K = 3
FFN_HIDDEN = 2816
FFN_CHUNKS = ((0, 1536), (1536, 2816))
RMS_EPS = 1e-6
NEG_INF = -1e30
LOG2E = math.log2(math.e)
LANES = 128
SUBLANES = 8
HALO = 16
HEADS_PER_SLAB = LANES // HEAD_DIM
F_TERMS = 3

BF16 = jnp.bfloat16
F32 = jnp.float32

TM_PROJ = 512
TM_MIX = 512
TM_FFN = 512
TQ = 512
TK = 512
CUMSUM_CHUNK = 256
QK_LOOKAHEAD = 4
ACC_ROWS = HEAD_DIM + SUBLANES
ACC_ROWS_PADDED = HEAD_DIM + 16
VMEM_LIMIT = 56 * 1024 * 1024


def _rms(x, w):
    return x * lax.rsqrt(jnp.mean(x * x, axis=-1, keepdims=True) + RMS_EPS) * w


def _const_spec(shape):
    nd = len(shape)
    return pl.BlockSpec(shape, lambda *_: (0,) * nd, pipeline_mode=pl.Buffered(1))


def _layer_spec(stacked, layer):
    rest = stacked.shape[1:]
    return pl.BlockSpec((None,) + rest, lambda *_: (layer,) + (0,) * len(rest),
                        pipeline_mode=pl.Buffered(1))


def _params(sem):
    return pltpu.CompilerParams(dimension_semantics=sem, vmem_limit_bytes=VMEM_LIMIT)


N_MAIN_CHUNKS = MAIN_COLS // BRANCH_W
FIRST_GATE_CHUNK = N_MAIN_CHUNKS - 3 * D_MODEL // BRANCH_W


def _prep_w_in_kernel(wt_ref, o_ref):
    i = pl.program_id(1)
    w = wt_ref[0].T
    scale = jnp.where(i == 0, LOG2E * HEAD_DIM ** -0.5,
                      jnp.where((i >= FIRST_GATE_CHUNK) & (i < N_MAIN_CHUNKS), 0.5, 1.0))
    col = lax.broadcasted_iota(jnp.int32, w.shape, 1)
    keep = (i < N_MAIN_CHUNKS) | (col < HEADS)
    o_ref[...] = jnp.where(keep, w * scale, 0.0).astype(BF16)


def _prep_w_in(w_in):
    depth, k, n = w_in.shape
    assert HEADS == SUBLANES and n == MAIN_COLS + HEADS
    f_chunk = 3 * BRANCH_W // SUBLANES

    def source(l, i):
        tile = jnp.where(i < N_MAIN_CHUNKS,
                         i * (BRANCH_W // SUBLANES) + jnp.where(i >= 3, 1, 0), f_chunk)
        return (l, tile * SUBLANES, 0)

    return pl.pallas_call(
        _prep_w_in_kernel,
        grid=(depth, N_MAIN_CHUNKS + 1),
        in_specs=[pl.BlockSpec((pl.Element(1), pl.Element(BRANCH_W), pl.Element(k)), source)],
        out_specs=pl.BlockSpec((None, k, BRANCH_W), lambda l, i: (l, 0, i)),
        out_shape=jax.ShapeDtypeStruct((depth, k, MAIN_COLS + BRANCH_W), BF16),
        compiler_params=_params(("parallel", "parallel")),
        name="prep_w_in",
    )(jnp.swapaxes(w_in, 1, 2))


def _inproj_kernel(x_ref, nw_ref, w_ref, bf_ref, bg_ref,
                   qt_ref, k_ref, vt_ref, u_ref, z_ref, cb_ref, gate_ref, lf_ref):
    h = _rms(x_ref[...], nw_ref[...]).astype(BF16)

    def proj(n):
        return jnp.dot(h, w_ref[:, n * BRANCH_W:(n + 1) * BRANCH_W], preferred_element_type=F32)

    qt = proj(0).T.astype(BF16)
    for c in range(qt_ref.shape[1]):
        qt_ref[0, c] = qt[:, c * TQ:(c + 1) * TQ]
    k_ref[...] = proj(1).astype(BF16)
    vt = proj(2).T.astype(BF16)
    for c in range(vt_ref.shape[1]):
        vt_ref[0, c] = vt[:, c * TK:(c + 1) * TK]
    u_ref[...] = proj(3).astype(BF16)
    cb_ref[...] = proj(5).astype(BF16)
    z_ref[...] = (proj(4) * proj(6)).astype(BF16)
    for n in range(3 * D_MODEL // BRANCH_W):
        cs = slice(n * BRANCH_W, (n + 1) * BRANCH_W)
        gate_ref[:, cs] = (1.0 + jnp.tanh(proj(7 + n) + bg_ref[:, cs])).astype(BF16)
    f = jnp.dot(h, w_ref[:, MAIN_COLS:MAIN_COLS + LANES], preferred_element_type=F32) + bf_ref[...]
    lf_ref[...] = jnp.minimum(f, 0.0) - jnp.log1p(jnp.exp(-jnp.abs(f)))


def _inproj(x2, layer, nw, w_main, b_f, b_g, batch, seq):
    n = x2.shape[0]
    tm = TM_PROJ
    tiles_per_seq = seq // tm
    row = lambda c: pl.BlockSpec((tm, c), lambda i: (i, 0))
    assert TQ == TK
    vt_spec = pl.BlockSpec((1, tm // TK, BRANCH_W, TK),
                           lambda i: (i // tiles_per_seq, i % tiles_per_seq, 0, 0))
    act = jax.ShapeDtypeStruct((n, BRANCH_W), BF16)
    act_t = jax.ShapeDtypeStruct((batch, seq // TK, BRANCH_W, TK), BF16)
    outs = [act_t, act, act_t, act, act, act,
            jax.ShapeDtypeStruct((n, 3 * D_MODEL), BF16), jax.ShapeDtypeStruct((n, LANES), F32)]
    return pl.pallas_call(
        _inproj_kernel,
        grid=(n // tm,),
        in_specs=[row(D_MODEL)] + [_layer_spec(t, layer) for t in (nw, w_main, b_f, b_g)],
        out_specs=[vt_spec, row(BRANCH_W), vt_spec] + [row(BRANCH_W)] * 3
                  + [row(3 * D_MODEL), row(LANES)],
        out_shape=outs,
        compiler_params=_params(("parallel",)),
        name="inproj",
    )(x2, nw, w_main, b_f, b_g)


def _cumsum_kernel(lf_ref, fqt_ref, ka_ref):
    c = CUMSUM_CHUNK
    r = lax.broadcasted_iota(jnp.int32, (c, c), 0)
    s = lax.broadcasted_iota(jnp.int32, (c, c), 1)
    tri = (s <= r).astype(BF16)

    src = lax.broadcasted_iota(jnp.int32, (F_TERMS * LANES, BRANCH_W), 0)
    dst = lax.broadcasted_iota(jnp.int32, (F_TERMS * LANES, BRANCH_W), 1)
    head, term = src % LANES, src // LANES
    want = (head // HEADS_PER_SLAB) * LANES + F_TERMS * (head % HEADS_PER_SLAB) + term
    place = ((head < HEADS) & (dst == want)).astype(BF16)

    def split(x):
        hi = x.astype(BF16)
        r1 = x - hi.astype(F32)
        mid = r1.astype(BF16)
        lo = (r1 - mid.astype(F32)).astype(BF16)
        return jnp.concatenate([hi, mid, lo], axis=1)

    carry = jnp.zeros((1, LANES), F32)
    for n in range(lf_ref.shape[0] // c):
        rows = slice(n * c, (n + 1) * c)
        parts = jnp.dot(tri, split(lf_ref[rows, :]), preferred_element_type=F32)
        cum = parts[:, :LANES] + parts[:, LANES:2 * LANES] + parts[:, 2 * LANES:] + carry
        carry = cum[c - 1:c, :]
        f2 = cum * LOG2E
        fqt_ref[0, :, rows] = f2.T[:HEADS, :]
        ka_ref[rows, :] = jnp.dot(split(-f2), place, preferred_element_type=F32).astype(BF16)


def _cumsum(lf, batch, seq):
    return pl.pallas_call(
        _cumsum_kernel,
        grid=(batch,),
        in_specs=[pl.BlockSpec((seq, LANES), lambda b: (b, 0))],
        out_specs=[pl.BlockSpec((1, HEADS, seq), lambda b: (b, 0, 0)),
                   pl.BlockSpec((seq, BRANCH_W), lambda b: (b, 0))],
        out_shape=[jax.ShapeDtypeStruct((batch, HEADS, seq), F32),
                   jax.ShapeDtypeStruct((batch * seq, BRANCH_W), BF16)],
        compiler_params=_params(("parallel",)),
        name="forget_cumsum",
    )(lf)


def _sublane_allreduce(x, op):
    shift = SUBLANES // 2
    while shift:
        x = op(x, pltpu.roll(x, shift, axis=0))
        shift //= 2
    return x


def _attn_kernel(qt_ref, k_ref, ka_ref, vt_ref, fq_ref, o_ref,
                 qa_ref, m_ref, acc_ref, *, tq, tk, nq):
    chan = lax.broadcasted_iota(jnp.int32, (LANES, 1), 0)
    ones_rows = jnp.ones((ACC_ROWS_PADDED - HEAD_DIM, tk), BF16)

    def begin_block(qi):
        for h in range(HEADS):
            slab, e = divmod(h, HEADS_PER_SLAB)
            q2 = qt_ref[0, qi, slab * LANES:(slab + 1) * LANES, :]
            qm = jnp.where(chan // HEAD_DIM == e, q2, jnp.zeros_like(q2))
            ones = ((chan >= F_TERMS * e) & (chan < F_TERMS * (e + 1))).astype(BF16)
            qa_ref[qi % 2, h] = jnp.concatenate([qm, jnp.broadcast_to(ones, (LANES, tq))], axis=0)
        m_ref[qi % 2] = jnp.full(m_ref.shape[1:], NEG_INF, F32)
        acc_ref[qi % 2] = jnp.zeros(acc_ref.shape[1:], F32)

    def logits_t(qi, j, h, nk, c0, nc):
        rows = slice(j * tk, j * tk + nk)
        cols = slice(h // HEADS_PER_SLAB * LANES, (h // HEADS_PER_SLAB + 1) * LANES)
        lhs = jnp.concatenate([k_ref[rows, cols], ka_ref[rows, cols]], axis=1)
        return jnp.dot(lhs, qa_ref[qi % 2, h, :, c0:c0 + nc],
                       preferred_element_type=F32)

    def softmax_pv(qi, j, h, nk, c0, nc, st):
        if j == qi:
            key_pos = lax.broadcasted_iota(jnp.int32, (nk, nc), 0)
            query_pos = c0 + lax.broadcasted_iota(jnp.int32, (nk, nc), 1)
            st = jnp.where(key_pos <= query_pos, st, NEG_INF)
        st3 = st.reshape(nk // SUBLANES, SUBLANES, nc)
        stat = slice(h * SUBLANES, (h + 1) * SUBLANES)
        qc = slice(c0, c0 + nc)
        fq = jnp.broadcast_to(fq_ref[0, h:h + 1, qi * tq + c0:qi * tq + c0 + nc], (SUBLANES, nc))
        m_old = m_ref[qi % 2, stat, qc]
        m_new = jnp.maximum(m_old, _sublane_allreduce(jnp.max(st3, axis=0), jnp.maximum) + fq)
        p = jnp.exp2(st3 - (m_new - fq)[None]).reshape(nk, nc).astype(BF16)
        alpha = jnp.exp2(m_old - m_new)
        m_ref[qi % 2, stat, qc] = m_new
        v_ones = jnp.concatenate([vt_ref[0, j, h * HEAD_DIM:(h + 1) * HEAD_DIM, :nk],
                                  ones_rows[:, :nk]], axis=0)
        pv = jnp.dot(v_ones, p, preferred_element_type=F32)[:ACC_ROWS]
        acc3 = acc_ref[qi % 2, h, :, qc].reshape(ACC_ROWS // SUBLANES, SUBLANES, nc)
        acc_ref[qi % 2, h, :, qc] = (acc3 * alpha[None]).reshape(ACC_ROWS, nc) + pv

    def end_block(qi):
        outs = []
        for h in range(HEADS):
            acc3 = acc_ref[qi % 2, h].reshape(ACC_ROWS // SUBLANES, SUBLANES, tq)
            outs.append((acc3[:HEAD_DIM // SUBLANES] / acc3[HEAD_DIM // SUBLANES][None])
                        .reshape(HEAD_DIM, tq))
        o_ref[qi * tq:(qi + 1) * tq, :] = jnp.concatenate(outs, axis=0).T.astype(BF16)

    items = []
    for qi in range(nq):
        for j in range(qi + 1):
            tiles = [(tk, 0, tq)] if j < qi else [(tk // 2, 0, tq // 2), (tk, tq // 2, tq // 2)]
            items += [(qi, j, h) + tile for h in range(HEADS) for tile in tiles]
    begun, pending = set(), []

    def issue(n):
        qi = items[n][0]
        if qi not in begun:
            begun.add(qi)
            begin_block(qi)
        pending.append(logits_t(*items[n]))

    for n in range(min(QK_LOOKAHEAD, len(items))):
        issue(n)
    for n, item in enumerate(items):
        if n + QK_LOOKAHEAD < len(items):
            issue(n + QK_LOOKAHEAD)
        softmax_pv(*item, pending.pop(0))
        if n + 1 == len(items) or items[n + 1][0] != item[0]:
            end_block(item[0])


def _attention(qt, k, ka, vt, fqt, batch, seq):
    tq, tk = TQ, TK
    assert tq == tk
    nq = seq // tq
    w = HEADS * HEAD_DIM
    return pl.pallas_call(
        functools.partial(_attn_kernel, tq=tq, tk=tk, nq=nq),
        grid=(batch,),
        in_specs=[pl.BlockSpec((1, nq, w, tq), lambda b: (b, 0, 0, 0)),
                  pl.BlockSpec((seq, w), lambda b: (b, 0)),
                  pl.BlockSpec((seq, w), lambda b: (b, 0)),
                  pl.BlockSpec((1, seq // tk, w, tk), lambda b: (b, 0, 0, 0)),
                  pl.BlockSpec((1, HEADS, seq), lambda b: (b, 0, 0))],
        out_specs=pl.BlockSpec((seq, w), lambda b: (b, 0)),
        out_shape=jax.ShapeDtypeStruct(k.shape, BF16),
        scratch_shapes=[pltpu.VMEM((2, HEADS, 2 * LANES, tq), BF16),
                        pltpu.VMEM((2, HEADS * SUBLANES, tq), F32),
                        pltpu.VMEM((2, HEADS, ACC_ROWS, tq), F32)],
        compiler_params=_params(("parallel",)),
        name="fox_attention",
    )(qt, k, ka, vt, fqt)


def _mix_kernel(a_ref, u_ref, up_ref, z_ref, zp_ref, cb_ref, gate_ref, x_ref,
                wpa_ref, pw_ref, ps_ref, wpp_ref, cw_ref, wpc_ref, wo_ref,
                o_ref, ue_ref, ze_ref, *, tm, tiles_per_seq):
    i = pl.program_id(0) % tiles_per_seq
    has_prev = i != 0
    pad, first = SUBLANES, SUBLANES + HALO

    u = u_ref[...].astype(F32)
    ue_ref[0:pad, :] = jnp.zeros((pad, BRANCH_W), F32)
    ue_ref[pad:first, :] = jnp.where(has_prev, up_ref[...].astype(F32), 0.0)
    ue_ref[first:, :] = u
    ze_ref[0:HALO, :] = jnp.where(has_prev, zp_ref[...].astype(F32), 0.0)
    ze_ref[HALO:, :] = z_ref[...].astype(F32)

    y_a = jnp.dot(a_ref[...], wpa_ref[...], preferred_element_type=F32)

    for level in range(len(POOL_WINDOWS) - 1):
        lag = 2 ** level
        cs = slice(level * POOL_GROUP_DIM, BRANCH_W)
        ue_ref[pad:, cs] = ue_ref[pad:, cs] + ue_ref[pad - lag:pad - lag + tm + HALO, cs]
    pos = i * tm + lax.broadcasted_iota(jnp.int32, (tm, 1), 0)
    ys = []
    for gi, w in enumerate(POOL_WINDOWS):
        cs = slice(gi * POOL_GROUP_DIM, (gi + 1) * POOL_GROUP_DIM)
        win = ue_ref[first:, cs]
        if gi == len(POOL_WINDOWS) - 1:
            win = win + ue_ref[first - w // 2:first - w // 2 + tm, cs]
        n_avail = jnp.minimum(pos + 1, w).astype(F32)
        d = win / n_avail - u[:, cs]
        y = jnp.dot(d.astype(BF16), pw_ref[gi], preferred_element_type=F32)
        ys.append((y * ps_ref[:, cs]).astype(BF16))
    y_b = jnp.dot(jnp.concatenate(ys, axis=1), wpp_ref[...], preferred_element_type=F32)

    conv = cw_ref[CONV_K - 1:CONV_K, :] * ze_ref[HALO:, :]
    for lag in range(1, CONV_K):
        conv = conv + cw_ref[CONV_K - 1 - lag:CONV_K - lag, :] * ze_ref[HALO - lag:HALO - lag + tm, :]
    y_c = jnp.dot((cb_ref[...].astype(F32) * conv).astype(BF16), wpc_ref[...],
                  preferred_element_type=F32)

    mixed = None
    for n, y in enumerate((y_a, y_b, y_c)):
        gate2 = gate_ref[:, n * D_MODEL:(n + 1) * D_MODEL].astype(F32)
        mixed = gate2 * y if mixed is None else mixed + gate2 * y
    o_ref[...] = x_ref[...] + jnp.dot(mixed.astype(BF16), wo_ref[...], preferred_element_type=F32)


def _mix(a, u, z, cb, gate, x2, layer, wpa, pw, ps, wpp, cw, wpc, wo, seq):
    n = x2.shape[0]
    tm = TM_MIX
    tiles_per_seq = seq // tm
    row = lambda c: pl.BlockSpec((tm, c), lambda i: (i, 0))
    prev = pl.BlockSpec((HALO, BRANCH_W), lambda i: (jnp.maximum(i * (tm // HALO) - 1, 0), 0))
    return pl.pallas_call(
        functools.partial(_mix_kernel, tm=tm, tiles_per_seq=tiles_per_seq),
        grid=(n // tm,),
        in_specs=[row(BRANCH_W), row(BRANCH_W), prev, row(BRANCH_W), prev,
                  row(BRANCH_W), row(3 * D_MODEL), row(D_MODEL)]
                 + [_layer_spec(t, layer) for t in (wpa, pw, ps, wpp, cw, wpc, wo)],
        out_specs=row(D_MODEL),
        out_shape=jax.ShapeDtypeStruct(x2.shape, F32),
        scratch_shapes=[pltpu.VMEM((SUBLANES + HALO + tm, BRANCH_W), F32),
                        pltpu.VMEM((HALO + tm, BRANCH_W), F32)],
        compiler_params=_params(("parallel",)),
        name="branch_mix",
    )(a, u, u, z, z, cb, gate, x2, wpa, pw, ps, wpp, cw, wpc, wo)


def _ffn_kernel(x_ref, nw_ref, wgu_ref, wd_ref, fn_ref, o_ref, *, final):
    x = x_ref[...]
    h = _rms(x, nw_ref[...]).astype(BF16)
    acc = x
    for lo, hi in FFN_CHUNKS:
        a = jnp.dot(h, wgu_ref[:, lo:hi], preferred_element_type=F32)
        b = jnp.dot(h, wgu_ref[:, FFN_HIDDEN + lo:FFN_HIDDEN + hi], preferred_element_type=F32)
        act = (a * jax.nn.sigmoid(a) * b).astype(BF16)
        acc = acc + jnp.dot(act, wd_ref[lo:hi, :], preferred_element_type=F32)
    if final:
        acc = _rms(acc, fn_ref[...])
    o_ref[...] = acc


def _ffn(x2, layer, nw, wgu, wd, fn, final):
    n = x2.shape[0]
    tm = TM_FFN
    row = pl.BlockSpec((tm, D_MODEL), lambda i: (i, 0))
    return pl.pallas_call(
        functools.partial(_ffn_kernel, final=final),
        grid=(n // tm,),
        in_specs=[row] + [_layer_spec(t, layer) for t in (nw, wgu, wd)] + [_const_spec(fn.shape)],
        out_specs=row,
        out_shape=jax.ShapeDtypeStruct(x2.shape, F32),
        compiler_params=_params(("parallel",)),
        name="swiglu_ffn",
    )(x2, nw, wgu, wd, fn)


def kernel(x, attn_norm, w_in, b_forget, b_gate, w_proj_attn, pool_w, pool_scale, w_proj_pool,
           conv_w, w_proj_conv, w_out, ffn_norm, w_gate_up, w_down, final_norm):
    batch, seq, d = x.shape
    depth = w_in.shape[0]
    x2 = x.reshape(batch * seq, d)
    w_main = _prep_w_in(w_in)
    b_f = jnp.pad(b_forget, ((0, 0), (0, LANES - HEADS))).reshape(depth, 1, LANES)
    b_g = (b_gate * 0.5).reshape(depth, 1, -1)
    w_pa, w_pp, w_pc = (w.astype(BF16) for w in (w_proj_attn, w_proj_pool, w_proj_conv))
    w_o = (w_out * 0.5).astype(BF16)
    w_pool = pool_w.astype(BF16)
    w_gu = w_gate_up.astype(BF16)
    w_d = w_down.astype(BF16)
    a_norm, f_norm = attn_norm.reshape(depth, 1, d), ffn_norm.reshape(depth, 1, d)
    p_scale = pool_scale.reshape(depth, 1, -1)
    fn = final_norm.reshape(1, d)
    for l in range(depth):
        qt, k, vt, u, z, cb, gate, lf = _inproj(x2, l, a_norm, w_main, b_f, b_g, batch, seq)
        fqt, ka = _cumsum(lf, batch, seq)
        a = _attention(qt, k, ka, vt, fqt, batch, seq)
        x2 = _mix(a, u, z, cb, gate, x2, l, w_pa, w_pool, p_scale, w_pp, conv_w, w_pc, w_o, seq)
        x2 = _ffn(x2, l, f_norm, w_gu, w_d, fn, final=(l == depth - 1))
    return x2.reshape(batch, seq, d)
```

```python
import functools
import math

import jax
import jax.numpy as jnp
from jax import lax
from jax.experimental import pallas as pl
from jax.experimental.pallas import tpu as pltpu

D_MODEL = 1024
HEADS = 8
HEAD_DIM = 64
BRANCH_W = 512
MAIN_COLS = 13 * BRANCH_W
POOL_WINDOWS = (2, 4, 8, 16)
POOL_GROUP_DIM = 128
CONV_K = 3
FFN_HIDDEN = 2816
FFN_CHUNKS = ((0, 1536), (1536, 2816))
RMS_EPS = 1e-6
NEG_INF = -1e30
LOG2E = math.log2(math.e)
LANES = 128
SUBLANES = 8
HALO = 16
HEADS_PER_SLAB = LANES // HEAD_DIM
F_TERMS = 3

BF16 = jnp.bfloat16
F32 = jnp.float32

TM_PROJ = 512
TM_MIX = 512
TM_FFN = 512
TQ = 512
TK = 512
CUMSUM_CHUNK = 256
QK_LOOKAHEAD = 4
ACC_ROWS = HEAD_DIM + SUBLANES
ACC_ROWS_PADDED = HEAD_DIM + 16
VMEM_LIMIT = 56 * 1024 * 1024


def _rms(x, w):
    return x * lax.rsqrt(jnp.mean(x * x, axis=-1, keepdims=True) + RMS_EPS) * w


def _const_spec(shape):
    nd = len(shape)
    return pl.BlockSpec(shape, lambda *_: (0,) * nd, pipeline_mode=pl.Buffered(1))


def _layer_spec(stacked, layer):
    rest = stacked.shape[1:]
    return pl.BlockSpec((None,) + rest, lambda *_: (layer,) + (0,) * len(rest),
                        pipeline_mode=pl.Buffered(1))


def _params(sem):
    return pltpu.CompilerParams(dimension_semantics=sem, vmem_limit_bytes=VMEM_LIMIT)


N_MAIN_CHUNKS = MAIN_COLS // BRANCH_W
FIRST_GATE_CHUNK = N_MAIN_CHUNKS - 3 * D_MODEL // BRANCH_W


def _prep_w_in_kernel(wt_ref, o_ref):
    i = pl.program_id(1)
    w = wt_ref[0].T
    scale = jnp.where(i == 0, LOG2E * HEAD_DIM ** -0.5,
                      jnp.where((i >= FIRST_GATE_CHUNK) & (i < N_MAIN_CHUNKS), 0.5, 1.0))
    col = lax.broadcasted_iota(jnp.int32, w.shape, 1)
    keep = (i < N_MAIN_CHUNKS) | (col < HEADS)
    o_ref[...] = jnp.where(keep, w * scale, 0.0).astype(BF16)


def _prep_w_in(w_in):
    depth, k, n = w_in.shape
    assert HEADS == SUBLANES and n == MAIN_COLS + HEADS
    f_chunk = 3 * BRANCH_W // SUBLANES

    def source(l, i):
        tile = jnp.where(i < N_MAIN_CHUNKS,
                         i * (BRANCH_W // SUBLANES) + jnp.where(i >= 3, 1, 0), f_chunk)
        return (l, tile * SUBLANES, 0)

    return pl.pallas_call(
        _prep_w_in_kernel,
        grid=(depth, N_MAIN_CHUNKS + 1),
        in_specs=[pl.BlockSpec((pl.Element(1), pl.Element(BRANCH_W), pl.Element(k)), source)],
        out_specs=pl.BlockSpec((None, k, BRANCH_W), lambda l, i: (l, 0, i)),
        out_shape=jax.ShapeDtypeStruct((depth, k, MAIN_COLS + BRANCH_W), BF16),
        compiler_params=_params(("parallel", "parallel")),
        name="prep_w_in",
    )(jnp.swapaxes(w_in, 1, 2))


def _inproj_kernel(x_ref, nw_ref, w_ref, bf_ref, bg_ref,
                   qt_ref, k_ref, vt_ref, u_ref, z_ref, cb_ref, gate_ref, lf_ref):
    h = _rms(x_ref[...], nw_ref[...]).astype(BF16)

    def proj(n):
        return jnp.dot(h, w_ref[:, n * BRANCH_W:(n + 1) * BRANCH_W], preferred_element_type=F32)

    qt = proj(0).T.astype(BF16)
    for c in range(qt_ref.shape[1]):
        qt_ref[0, c] = qt[:, c * TQ:(c + 1) * TQ]
    k_ref[...] = proj(1).astype(BF16)
    vt = proj(2).T.astype(BF16)
    for c in range(vt_ref.shape[1]):
        vt_ref[0, c] = vt[:, c * TK:(c + 1) * TK]
    u_ref[...] = proj(3).astype(BF16)
    cb_ref[...] = proj(5).astype(BF16)
    z_ref[...] = (proj(4) * proj(6)).astype(BF16)
    for n in range(3 * D_MODEL // BRANCH_W):
        cs = slice(n * BRANCH_W, (n + 1) * BRANCH_W)
        gate_ref[:, cs] = (1.0 + jnp.tanh(proj(7 + n) + bg_ref[:, cs])).astype(BF16)
    f = jnp.dot(h, w_ref[:, MAIN_COLS:MAIN_COLS + LANES], preferred_element_type=F32) + bf_ref[...]
    lf_ref[...] = jnp.minimum(f, 0.0) - jnp.log1p(jnp.exp(-jnp.abs(f)))


def _inproj(x2, layer, nw, w_main, b_f, b_g, batch, seq):
    n = x2.shape[0]
    tm = TM_PROJ
    tiles_per_seq = seq // tm
    row = lambda c: pl.BlockSpec((tm, c), lambda i: (i, 0))
    assert TQ == TK
    vt_spec = pl.BlockSpec((1, tm // TK, BRANCH_W, TK),
                           lambda i: (i // tiles_per_seq, i % tiles_per_seq, 0, 0))
    act = jax.ShapeDtypeStruct((n, BRANCH_W), BF16)
    act_t = jax.ShapeDtypeStruct((batch, seq // TK, BRANCH_W, TK), BF16)
    outs = [act_t, act, act_t, act, act, act,
            jax.ShapeDtypeStruct((n, 3 * D_MODEL), BF16), jax.ShapeDtypeStruct((n, LANES), F32)]
    return pl.pallas_call(
        _inproj_kernel,
        grid=(n // tm,),
        in_specs=[row(D_MODEL)] + [_layer_spec(t, layer) for t in (nw, w_main, b_f, b_g)],
        out_specs=[vt_spec, row(BRANCH_W), vt_spec] + [row(BRANCH_W)] * 3
                  + [row(3 * D_MODEL), row(LANES)],
        out_shape=outs,
        compiler_params=_params(("parallel",)),
        name="inproj",
    )(x2, nw, w_main, b_f, b_g)


def _cumsum_kernel(lf_ref, fqt_ref, ka_ref):
    c = CUMSUM_CHUNK
    r = lax.broadcasted_iota(jnp.int32, (c, c), 0)
    s = lax.broadcasted_iota(jnp.int32, (c, c), 1)
    tri = (s <= r).astype(BF16)

    src = lax.broadcasted_iota(jnp.int32, (F_TERMS * LANES, LANES), 0)
    dst = lax.broadcasted_iota(jnp.int32, (F_TERMS * LANES, LANES), 1)
    head, term = src % LANES, src // LANES
    place = ((head < HEADS) & (dst == F_TERMS * head + term)).astype(BF16)

    def split(x):
        hi = x.astype(BF16)
        r1 = x - hi.astype(F32)
        mid = r1.astype(BF16)
        lo = (r1 - mid.astype(F32)).astype(BF16)
        return jnp.concatenate([hi, mid, lo], axis=1)

    carry = jnp.zeros((1, LANES), F32)
    for n in range(lf_ref.shape[0] // c):
        rows = slice(n * c, (n + 1) * c)
        parts = jnp.dot(tri, split(lf_ref[rows, :]), preferred_element_type=F32)
        cum = parts[:, :LANES] + parts[:, LANES:2 * LANES] + parts[:, 2 * LANES:] + carry
        carry = cum[c - 1:c, :]
        f2 = cum * LOG2E
        fqt_ref[0, :, rows] = f2.T[:HEADS, :]
        ka_ref[rows, :] = jnp.dot(split(-f2), place, preferred_element_type=F32).astype(BF16)


def _cumsum(lf, batch, seq):
    return pl.pallas_call(
        _cumsum_kernel,
        grid=(batch,),
        in_specs=[pl.BlockSpec((seq, LANES), lambda b: (b, 0))],
        out_specs=[pl.BlockSpec((1, HEADS, seq), lambda b: (b, 0, 0)),
                   pl.BlockSpec((seq, LANES), lambda b: (b, 0))],
        out_shape=[jax.ShapeDtypeStruct((batch, HEADS, seq), F32),
                   jax.ShapeDtypeStruct((batch * seq, LANES), BF16)],
        compiler_params=_params(("parallel",)),
        name="forget_cumsum",
    )(lf)


def _sublane_allreduce(x, op):
    shift = SUBLANES // 2
    while shift:
        x = op(x, pltpu.roll(x, shift, axis=0))
        shift //= 2
    return x


def _attn_kernel(qt_ref, k_ref, ka_ref, vt_ref, fq_ref, o_ref,
                 qa_ref, m_ref, acc_ref, *, tq, tk, nq):
    chan = lax.broadcasted_iota(jnp.int32, (LANES, 1), 0)
    ones_rows = jnp.ones((ACC_ROWS_PADDED - HEAD_DIM, tk), BF16)

    def begin_block(qi):
        for h in range(HEADS):
            slab, e = divmod(h, HEADS_PER_SLAB)
            q2 = qt_ref[0, qi, slab * LANES:(slab + 1) * LANES, :]
            qm = jnp.where(chan // HEAD_DIM == e, q2, jnp.zeros_like(q2))
            ones = ((chan >= F_TERMS * h) & (chan < F_TERMS * (h + 1))).astype(BF16)
            qa_ref[qi % 2, h] = jnp.concatenate([qm, jnp.broadcast_to(ones, (LANES, tq))], axis=0)
        m_ref[qi % 2] = jnp.full(m_ref.shape[1:], NEG_INF, F32)
        acc_ref[qi % 2] = jnp.zeros(acc_ref.shape[1:], F32)

    def logits_t(qi, j, h, nk, c0, nc):
        rows = slice(j * tk, j * tk + nk)
        cols = slice(h // HEADS_PER_SLAB * LANES, (h // HEADS_PER_SLAB + 1) * LANES)
        lhs = jnp.concatenate([k_ref[rows, cols], ka_ref[rows, :]], axis=1)
        return jnp.dot(lhs, qa_ref[qi % 2, h, :, c0:c0 + nc],
                       preferred_element_type=F32)

    def softmax_pv(qi, j, h, nk, c0, nc, st):
        if j == qi:
            key_pos = lax.broadcasted_iota(jnp.int32, (nk, nc), 0)
            query_pos = c0 + lax.broadcasted_iota(jnp.int32, (nk, nc), 1)
            st = jnp.where(key_pos <= query_pos, st, NEG_INF)
        st3 = st.reshape(nk // SUBLANES, SUBLANES, nc)
        stat = slice(h * SUBLANES, (h + 1) * SUBLANES)
        qc = slice(c0, c0 + nc)
        fq = jnp.broadcast_to(fq_ref[0, h:h + 1, qi * tq + c0:qi * tq + c0 + nc], (SUBLANES, nc))
        m_old = m_ref[qi % 2, stat, qc]
        m_new = jnp.maximum(m_old, _sublane_allreduce(jnp.max(st3, axis=0), jnp.maximum) + fq)
        p = jnp.exp2(st3 - (m_new - fq)[None]).reshape(nk, nc).astype(BF16)
        alpha = jnp.exp2(m_old - m_new)
        m_ref[qi % 2, stat, qc] = m_new
        v_ones = jnp.concatenate([vt_ref[0, j, h * HEAD_DIM:(h + 1) * HEAD_DIM, :nk],
                                  ones_rows[:, :nk]], axis=0)
        pv = jnp.dot(v_ones, p, preferred_element_type=F32)[:ACC_ROWS]
        acc3 = acc_ref[qi % 2, h, :, qc].reshape(ACC_ROWS // SUBLANES, SUBLANES, nc)
        acc_ref[qi % 2, h, :, qc] = (acc3 * alpha[None]).reshape(ACC_ROWS, nc) + pv

    def end_block(qi):
        outs = []
        for h in range(HEADS):
            acc3 = acc_ref[qi % 2, h].reshape(ACC_ROWS // SUBLANES, SUBLANES, tq)
            outs.append((acc3[:HEAD_DIM // SUBLANES] / acc3[HEAD_DIM // SUBLANES][None])
                        .reshape(HEAD_DIM, tq))
        o_ref[qi * tq:(qi + 1) * tq, :] = jnp.concatenate(outs, axis=0).T.astype(BF16)

    items = []
    for qi in range(nq):
        for j in range(qi + 1):
            tiles = [(tk, 0, tq)] if j < qi else [(tk // 2, 0, tq // 2), (tk, tq // 2, tq // 2)]
            items += [(qi, j, h) + tile for h in range(HEADS) for tile in tiles]
    begun, pending = set(), []

    def issue(n):
        qi = items[n][0]
        if qi not in begun:
            begun.add(qi)
            begin_block(qi)
        pending.append(logits_t(*items[n]))

    for n in range(min(QK_LOOKAHEAD, len(items))):
        issue(n)
    for n, item in enumerate(items):
        if n + QK_LOOKAHEAD < len(items):
            issue(n + QK_LOOKAHEAD)
        softmax_pv(*item, pending.pop(0))
        if n + 1 == len(items) or items[n + 1][0] != item[0]:
            end_block(item[0])


def _attention(qt, k, ka, vt, fqt, batch, seq):
    tq, tk = TQ, TK
    assert tq == tk
    nq = seq // tq
    w = HEADS * HEAD_DIM
    return pl.pallas_call(
        functools.partial(_attn_kernel, tq=tq, tk=tk, nq=nq),
        grid=(batch,),
        in_specs=[pl.BlockSpec((1, nq, w, tq), lambda b: (b, 0, 0, 0)),
                  pl.BlockSpec((seq, w), lambda b: (b, 0)),
                  pl.BlockSpec((seq, LANES), lambda b: (b, 0)),
                  pl.BlockSpec((1, seq // tk, w, tk), lambda b: (b, 0, 0, 0)),
                  pl.BlockSpec((1, HEADS, seq), lambda b: (b, 0, 0))],
        out_specs=pl.BlockSpec((seq, w), lambda b: (b, 0)),
        out_shape=jax.ShapeDtypeStruct(k.shape, BF16),
        scratch_shapes=[pltpu.VMEM((2, HEADS, 2 * LANES, tq), BF16),
                        pltpu.VMEM((2, HEADS * SUBLANES, tq), F32),
                        pltpu.VMEM((2, HEADS, ACC_ROWS, tq), F32)],
        compiler_params=_params(("parallel",)),
        name="fox_attention",
    )(qt, k, ka, vt, fqt)


def _mix_kernel(a_ref, u_ref, up_ref, z_ref, zp_ref, cb_ref, gate_ref, x_ref,
                wpa_ref, pw_ref, ps_ref, wpp_ref, cw_ref, wpc_ref, wo_ref,
                o_ref, ue_ref, ze_ref, *, tm, tiles_per_seq):
    i = pl.program_id(0) % tiles_per_seq
    has_prev = i != 0
    pad, first = SUBLANES, SUBLANES + HALO

    u = u_ref[...].astype(F32)
    ue_ref[0:pad, :] = jnp.zeros((pad, BRANCH_W), F32)
    ue_ref[pad:first, :] = jnp.where(has_prev, up_ref[...].astype(F32), 0.0)
    ue_ref[first:, :] = u
    ze_ref[0:HALO, :] = jnp.where(has_prev, zp_ref[...].astype(F32), 0.0)
    ze_ref[HALO:, :] = z_ref[...].astype(F32)

    y_a = jnp.dot(a_ref[...], wpa_ref[...], preferred_element_type=F32)

    for level in range(len(POOL_WINDOWS) - 1):
        lag = 2 ** level
        cs = slice(level * POOL_GROUP_DIM, BRANCH_W)
        ue_ref[pad:, cs] = ue_ref[pad:, cs] + ue_ref[pad - lag:pad - lag + tm + HALO, cs]
    pos = i * tm + lax.broadcasted_iota(jnp.int32, (tm, 1), 0)
    ys = []
    for gi, w in enumerate(POOL_WINDOWS):
        cs = slice(gi * POOL_GROUP_DIM, (gi + 1) * POOL_GROUP_DIM)
        win = ue_ref[first:, cs]
        if gi == len(POOL_WINDOWS) - 1:
            win = win + ue_ref[first - w // 2:first - w // 2 + tm, cs]
        n_avail = jnp.minimum(pos + 1, w).astype(F32)
        d = win / n_avail - u[:, cs]
        y = jnp.dot(d.astype(BF16), pw_ref[gi], preferred_element_type=F32)
        ys.append((y * ps_ref[:, cs]).astype(BF16))
    y_b = jnp.dot(jnp.concatenate(ys, axis=1), wpp_ref[...], preferred_element_type=F32)

    conv = cw_ref[CONV_K - 1:CONV_K, :] * ze_ref[HALO:, :]
    for lag in range(1, CONV_K):
        conv = conv + cw_ref[CONV_K - 1 - lag:CONV_K - lag, :] * ze_ref[HALO - lag:HALO - lag + tm, :]
    y_c = jnp.dot((cb_ref[...].astype(F32) * conv).astype(BF16), wpc_ref[...],
                  preferred_element_type=F32)

    mixed = None
    for n, y in enumerate((y_a, y_b, y_c)):
        gate2 = gate_ref[:, n * D_MODEL:(n + 1) * D_MODEL].astype(F32)
        mixed = gate2 * y if mixed is None else mixed + gate2 * y
    o_ref[...] = x_ref[...] + jnp.dot(mixed.astype(BF16), wo_ref[...], preferred_element_type=F32)


def _mix(a, u, z, cb, gate, x2, layer, wpa, pw, ps, wpp, cw, wpc, wo, seq):
    n = x2.shape[0]
    tm = TM_MIX
    tiles_per_seq = seq // tm
    row = lambda c: pl.BlockSpec((tm, c), lambda i: (i, 0))
    prev = pl.BlockSpec((HALO, BRANCH_W), lambda i: (jnp.maximum(i * (tm // HALO) - 1, 0), 0))
    return pl.pallas_call(
        functools.partial(_mix_kernel, tm=tm, tiles_per_seq=tiles_per_seq),
        grid=(n // tm,),
        in_specs=[row(BRANCH_W), row(BRANCH_W), prev, row(BRANCH_W), prev,
                  row(BRANCH_W), row(3 * D_MODEL), row(D_MODEL)]
                 + [_layer_spec(t, layer) for t in (wpa, pw, ps, wpp, cw, wpc, wo)],
        out_specs=row(D_MODEL),
        out_shape=jax.ShapeDtypeStruct(x2.shape, F32),
        scratch_shapes=[pltpu.VMEM((SUBLANES + HALO + tm, BRANCH_W), F32),
                        pltpu.VMEM((HALO + tm, BRANCH_W), F32)],
        compiler_params=_params(("parallel",)),
        name="branch_mix",
    )(a, u, u, z, z, cb, gate, x2, wpa, pw, ps, wpp, cw, wpc, wo)


def _ffn_kernel(x_ref, nw_ref, wgu_ref, wd_ref, fn_ref, o_ref, *, final):
    x = x_ref[...]
    h = _rms(x, nw_ref[...]).astype(BF16)
    acc = x
    for lo, hi in FFN_CHUNKS:
        a = jnp.dot(h, wgu_ref[:, lo:hi], preferred_element_type=F32)
        b = jnp.dot(h, wgu_ref[:, FFN_HIDDEN + lo:FFN_HIDDEN + hi], preferred_element_type=F32)
        act = (a * jax.nn.sigmoid(a) * b).astype(BF16)
        acc = acc + jnp.dot(act, wd_ref[lo:hi, :], preferred_element_type=F32)
    if final:
        acc = _rms(acc, fn_ref[...])
    o_ref[...] = acc


def _ffn(x2, layer, nw, wgu, wd, fn, final):
    n = x2.shape[0]
    tm = TM_FFN
    row = pl.BlockSpec((tm, D_MODEL), lambda i: (i, 0))
    return pl.pallas_call(
        functools.partial(_ffn_kernel, final=final),
        grid=(n // tm,),
        in_specs=[row] + [_layer_spec(t, layer) for t in (nw, wgu, wd)] + [_const_spec(fn.shape)],
        out_specs=row,
        out_shape=jax.ShapeDtypeStruct(x2.shape, F32),
        compiler_params=_params(("parallel",)),
        name="swiglu_ffn",
    )(x2, nw, wgu, wd, fn)


def kernel(x, attn_norm, w_in, b_forget, b_gate, w_proj_attn, pool_w, pool_scale, w_proj_pool,
           conv_w, w_proj_conv, w_out, ffn_norm, w_gate_up, w_down, final_norm):
    batch, seq, d = x.shape
    depth = w_in.shape[0]
    x2 = x.reshape(batch * seq, d)
    w_main = _prep_w_in(w_in)
    b_f = jnp.pad(b_forget, ((0, 0), (0, LANES - HEADS))).reshape(depth, 1, LANES)
    b_g = (b_gate * 0.5).reshape(depth, 1, -1)
    w_pa, w_pp, w_pc = (w.astype(BF16) for w in (w_proj_attn, w_proj_pool, w_proj_conv))
    w_o = (w_out * 0.5).astype(BF16)
    w_pool = pool_w.astype(BF16)
    w_gu = w_gate_up.astype(BF16)
    w_d = w_down.astype(BF16)
    a_norm, f_norm = attn_norm.reshape(depth, 1, d), ffn_norm.reshape(depth, 1, d)
    p_scale = pool_scale.reshape(depth, 1, -1)
    fn = final_norm.reshape(1, d)
    for l in range(depth):
        qt, k, vt, u, z, cb, gate, lf = _inproj(x2, l, a_norm, w_main, b_f, b_g, batch, seq)
        fqt, ka = _cumsum(lf, batch, seq)
        a = _attention(qt, k, ka, vt, fqt, batch, seq)
        x2 = _mix(a, u, z, cb, gate, x2, l, w_pa, w_pool, p_scale, w_pp, conv_w, w_pc, w_o, seq)
        x2 = _ffn(x2, l, f_norm, w_gu, w_d, fn, final=(l == depth - 1))
    return x2.reshape(batch, seq, d)
```

```python
import functools
import math

import jax
import jax.numpy as jnp
from jax import lax
from jax.experimental import pallas as pl
from jax.experimental.pallas import tpu as pltpu

D_MODEL = 1024
HEADS = 8
HEAD_DIM = 64
BRANCH_W = 512
MAIN_COLS = 13 * BRANCH_W
POOL_WINDOWS = (2, 4, 8, 16)
POOL_GROUP_DIM = 128
CONV_K = 3
FFN_HIDDEN = 2816
FFN_CHUNKS = ((0, 1536), (1536, 2816))
RMS_EPS = 1e-6
NEG_INF = -1e30
LOG2E = math.log2(math.e)
LANES = 128
SUBLANES = 8
HALO = 16
HEADS_PER_SLAB = LANES // HEAD_DIM
F_TERMS = 3

BF16 = jnp.bfloat16
F32 = jnp.float32

TM_PROJ = 512
TM_MIX = 512
TM_FFN = 512
FFN_SUBTILES = 2
TQ = 512
TK = 512
CUMSUM_CHUNK = 256
QK_LOOKAHEAD = 4
ACC_ROWS = HEAD_DIM + SUBLANES
ACC_ROWS_PADDED = HEAD_DIM + 16
VMEM_LIMIT = 56 * 1024 * 1024


def _rms(x, w):
    return x * lax.rsqrt(jnp.mean(x * x, axis=-1, keepdims=True) + RMS_EPS) * w


def _const_spec(shape):
    nd = len(shape)
    return pl.BlockSpec(shape, lambda *_: (0,) * nd, pipeline_mode=pl.Buffered(1))


def _layer_spec(stacked, layer):
    rest = stacked.shape[1:]
    return pl.BlockSpec((None,) + rest, lambda *_: (layer,) + (0,) * len(rest),
                        pipeline_mode=pl.Buffered(1))


def _params(sem):
    return pltpu.CompilerParams(dimension_semantics=sem, vmem_limit_bytes=VMEM_LIMIT)


N_MAIN_CHUNKS = MAIN_COLS // BRANCH_W
FIRST_GATE_CHUNK = N_MAIN_CHUNKS - 3 * D_MODEL // BRANCH_W


def _prep_w_in_kernel(wt_ref, o_ref):
    i = pl.program_id(1)
    w = wt_ref[0].T
    scale = jnp.where(i == 0, LOG2E * HEAD_DIM ** -0.5,
                      jnp.where((i >= FIRST_GATE_CHUNK) & (i < N_MAIN_CHUNKS), 0.5, 1.0))
    col = lax.broadcasted_iota(jnp.int32, w.shape, 1)
    keep = (i < N_MAIN_CHUNKS) | (col < HEADS)
    o_ref[...] = jnp.where(keep, w * scale, 0.0).astype(BF16)


def _prep_w_in(w_in):
    depth, k, n = w_in.shape
    assert HEADS == SUBLANES and n == MAIN_COLS + HEADS
    f_chunk = 3 * BRANCH_W // SUBLANES

    def source(l, i):
        tile = jnp.where(i < N_MAIN_CHUNKS,
                         i * (BRANCH_W // SUBLANES) + jnp.where(i >= 3, 1, 0), f_chunk)
        return (l, tile * SUBLANES, 0)

    return pl.pallas_call(
        _prep_w_in_kernel,
        grid=(depth, N_MAIN_CHUNKS + 1),
        in_specs=[pl.BlockSpec((pl.Element(1), pl.Element(BRANCH_W), pl.Element(k)), source)],
        out_specs=pl.BlockSpec((None, k, BRANCH_W), lambda l, i: (l, 0, i)),
        out_shape=jax.ShapeDtypeStruct((depth, k, MAIN_COLS + BRANCH_W), BF16),
        compiler_params=_params(("parallel", "parallel")),
        name="prep_w_in",
    )(jnp.swapaxes(w_in, 1, 2))


def _inproj_kernel(x0_ref, xn_ref, nw_ref, w_ref, bf_ref, bg_ref,
                   qt_ref, k_ref, vt_ref, u_ref, z_ref, cb_ref, gate_ref, lf_ref, ha_ref, hb_ref):
    i = pl.program_id(0)

    @pl.when(i == 0)
    def _():
        ha_ref[...] = _rms(x0_ref[...], nw_ref[...]).astype(BF16)

    def tile(h_ref, hn_ref):
        hn_ref[...] = _rms(xn_ref[...], nw_ref[...]).astype(BF16)

        def proj(n):
            return jnp.dot(h_ref[...], w_ref[:, n * BRANCH_W:(n + 1) * BRANCH_W],
                           preferred_element_type=F32)

        f = jnp.dot(h_ref[...], w_ref[:, MAIN_COLS:MAIN_COLS + LANES],
                    preferred_element_type=F32) + bf_ref[...]
        lf_ref[...] = jnp.minimum(f, 0.0) - jnp.log1p(jnp.exp(-jnp.abs(f)))
        qt = proj(0).T.astype(BF16)
        for c in range(qt_ref.shape[1]):
            qt_ref[0, c] = qt[:, c * TQ:(c + 1) * TQ]
        vt = proj(2).T.astype(BF16)
        for c in range(vt_ref.shape[1]):
            vt_ref[0, c] = vt[:, c * TK:(c + 1) * TK]
        for n in range(3 * D_MODEL // BRANCH_W):
            cs = slice(n * BRANCH_W, (n + 1) * BRANCH_W)
            gate_ref[:, cs] = (1.0 + jnp.tanh(proj(7 + n) + bg_ref[:, cs])).astype(BF16)
        z_ref[...] = (proj(4) * proj(6)).astype(BF16)
        u_ref[...] = proj(3).astype(BF16)
        cb_ref[...] = proj(5).astype(BF16)
        k_ref[...] = proj(1).astype(BF16)

    @pl.when(i % 2 == 0)
    def _():
        tile(ha_ref, hb_ref)

    @pl.when(i % 2 == 1)
    def _():
        tile(hb_ref, ha_ref)


def _inproj(x2, layer, nw, w_main, b_f, b_g, batch, seq):
    n = x2.shape[0]
    tm = TM_PROJ
    tiles_per_seq = seq // tm
    row = lambda c: pl.BlockSpec((tm, c), lambda i: (i, 0))
    assert TQ == TK
    vt_spec = pl.BlockSpec((1, tm // TK, BRANCH_W, TK),
                           lambda i: (i // tiles_per_seq, i % tiles_per_seq, 0, 0))
    act = jax.ShapeDtypeStruct((n, BRANCH_W), BF16)
    act_t = jax.ShapeDtypeStruct((batch, seq // TK, BRANCH_W, TK), BF16)
    outs = [act_t, act, act_t, act, act, act,
            jax.ShapeDtypeStruct((n, 3 * D_MODEL), BF16), jax.ShapeDtypeStruct((n, LANES), F32)]
    return pl.pallas_call(
        _inproj_kernel,
        grid=(n // tm,),
        in_specs=[pl.BlockSpec((tm, D_MODEL), lambda i: (0, 0)),
                  pl.BlockSpec((tm, D_MODEL), lambda i: (jnp.minimum(i + 1, n // tm - 1), 0))]
                 + [_layer_spec(t, layer) for t in (nw, w_main, b_f, b_g)],
        out_specs=[vt_spec, row(BRANCH_W), vt_spec] + [row(BRANCH_W)] * 3
                  + [row(3 * D_MODEL), row(LANES)],
        out_shape=outs,
        scratch_shapes=[pltpu.VMEM((tm, D_MODEL), BF16), pltpu.VMEM((tm, D_MODEL), BF16)],
        compiler_params=_params(("arbitrary",)),
        name="inproj",
    )(x2, x2, nw, w_main, b_f, b_g)


def _cumsum_kernel(lf_ref, fqt_ref, ka_ref):
    c = CUMSUM_CHUNK
    r = lax.broadcasted_iota(jnp.int32, (c, c), 0)
    s = lax.broadcasted_iota(jnp.int32, (c, c), 1)
    tri = (s <= r).astype(BF16)

    src = lax.broadcasted_iota(jnp.int32, (F_TERMS * LANES, LANES), 0)
    dst = lax.broadcasted_iota(jnp.int32, (F_TERMS * LANES, LANES), 1)
    head, term = src % LANES, src // LANES
    place = ((head < HEADS) & (dst == F_TERMS * head + term)).astype(BF16)

    def split(x):
        hi = x.astype(BF16)
        r1 = x - hi.astype(F32)
        mid = r1.astype(BF16)
        lo = (r1 - mid.astype(F32)).astype(BF16)
        return jnp.concatenate([hi, mid, lo], axis=1)

    carry = jnp.zeros((1, LANES), F32)
    for n in range(lf_ref.shape[0] // c):
        rows = slice(n * c, (n + 1) * c)
        parts = jnp.dot(tri, split(lf_ref[rows, :]), preferred_element_type=F32)
        cum = parts[:, :LANES] + parts[:, LANES:2 * LANES] + parts[:, 2 * LANES:] + carry
        carry = cum[c - 1:c, :]
        f2 = cum * LOG2E
        fqt_ref[0, :, rows] = f2.T[:HEADS, :]
        ka_ref[rows, :] = jnp.dot(split(-f2), place, preferred_element_type=F32).astype(BF16)


def _cumsum(lf, batch, seq):
    return pl.pallas_call(
        _cumsum_kernel,
        grid=(batch,),
        in_specs=[pl.BlockSpec((seq, LANES), lambda b: (b, 0))],
        out_specs=[pl.BlockSpec((1, HEADS, seq), lambda b: (b, 0, 0)),
                   pl.BlockSpec((seq, LANES), lambda b: (b, 0))],
        out_shape=[jax.ShapeDtypeStruct((batch, HEADS, seq), F32),
                   jax.ShapeDtypeStruct((batch * seq, LANES), BF16)],
        compiler_params=_params(("parallel",)),
        name="forget_cumsum",
    )(lf)


def _sublane_allreduce(x, op):
    shift = SUBLANES // 2
    while shift:
        x = op(x, pltpu.roll(x, shift, axis=0))
        shift //= 2
    return x


def _attn_kernel(qt_ref, k_ref, ka_ref, vt_ref, fq_ref, o_ref,
                 qa_ref, m_ref, acc_ref, *, tq, tk, nq):
    chan = lax.broadcasted_iota(jnp.int32, (LANES, 1), 0)
    ones_rows = jnp.ones((ACC_ROWS_PADDED - HEAD_DIM, tk), BF16)

    def begin_block(qi):
        for h in range(HEADS):
            slab, e = divmod(h, HEADS_PER_SLAB)
            q2 = qt_ref[0, qi, slab * LANES:(slab + 1) * LANES, :]
            qm = jnp.where(chan // HEAD_DIM == e, q2, jnp.zeros_like(q2))
            ones = ((chan >= F_TERMS * h) & (chan < F_TERMS * (h + 1))).astype(BF16)
            qa_ref[qi % 2, h] = jnp.concatenate([qm, jnp.broadcast_to(ones, (LANES, tq))], axis=0)
        m_ref[qi % 2] = jnp.full(m_ref.shape[1:], NEG_INF, F32)
        acc_ref[qi % 2] = jnp.zeros(acc_ref.shape[1:], F32)

    def logits_t(qi, j, h, nk, c0, nc):
        rows = slice(j * tk, j * tk + nk)
        cols = slice(h // HEADS_PER_SLAB * LANES, (h // HEADS_PER_SLAB + 1) * LANES)
        lhs = jnp.concatenate([k_ref[rows, cols], ka_ref[rows, :]], axis=1)
        return jnp.dot(lhs, qa_ref[qi % 2, h, :, c0:c0 + nc],
                       preferred_element_type=F32)

    def softmax_pv(qi, j, h, nk, c0, nc, st):
        if j == qi:
            key_pos = lax.broadcasted_iota(jnp.int32, (nk, nc), 0)
            query_pos = c0 + lax.broadcasted_iota(jnp.int32, (nk, nc), 1)
            st = jnp.where(key_pos <= query_pos, st, NEG_INF)
        st3 = st.reshape(nk // SUBLANES, SUBLANES, nc)
        stat = slice(h * SUBLANES, (h + 1) * SUBLANES)
        qc = slice(c0, c0 + nc)
        fq = jnp.broadcast_to(fq_ref[0, h:h + 1, qi * tq + c0:qi * tq + c0 + nc], (SUBLANES, nc))
        m_old = m_ref[qi % 2, stat, qc]
        m_new = jnp.maximum(m_old, _sublane_allreduce(jnp.max(st3, axis=0), jnp.maximum) + fq)
        p = jnp.exp2(st3 - (m_new - fq)[None]).reshape(nk, nc).astype(BF16)
        alpha = jnp.exp2(m_old - m_new)
        m_ref[qi % 2, stat, qc] = m_new
        v_ones = jnp.concatenate([vt_ref[0, j, h * HEAD_DIM:(h + 1) * HEAD_DIM, :nk],
                                  ones_rows[:, :nk]], axis=0)
        pv = jnp.dot(v_ones, p, preferred_element_type=F32)[:ACC_ROWS]
        acc3 = acc_ref[qi % 2, h, :, qc].reshape(ACC_ROWS // SUBLANES, SUBLANES, nc)
        acc_ref[qi % 2, h, :, qc] = (acc3 * alpha[None]).reshape(ACC_ROWS, nc) + pv

    def end_block(qi):
        outs = []
        for h in range(HEADS):
            acc3 = acc_ref[qi % 2, h].reshape(ACC_ROWS // SUBLANES, SUBLANES, tq)
            outs.append((acc3[:HEAD_DIM // SUBLANES] / acc3[HEAD_DIM // SUBLANES][None])
                        .reshape(HEAD_DIM, tq))
        o_ref[qi * tq:(qi + 1) * tq, :] = jnp.concatenate(outs, axis=0).T.astype(BF16)

    items = []
    for qi in range(nq):
        for j in range(qi + 1):
            tiles = [(tk, 0, tq)] if j < qi else [(tk // 2, 0, tq // 2), (tk, tq // 2, tq // 2)]
            items += [(qi, j, h) + tile for h in range(HEADS) for tile in tiles]
    begun, pending = set(), []

    def issue(n):
        qi = items[n][0]
        if qi not in begun:
            begun.add(qi)
            begin_block(qi)
        pending.append(logits_t(*items[n]))

    for n in range(min(QK_LOOKAHEAD, len(items))):
        issue(n)
    for n, item in enumerate(items):
        if n + QK_LOOKAHEAD < len(items):
            issue(n + QK_LOOKAHEAD)
        softmax_pv(*item, pending.pop(0))
        if n + 1 == len(items) or items[n + 1][0] != item[0]:
            end_block(item[0])


def _attention(qt, k, ka, vt, fqt, batch, seq):
    tq, tk = TQ, TK
    assert tq == tk
    nq = seq // tq
    w = HEADS * HEAD_DIM
    return pl.pallas_call(
        functools.partial(_attn_kernel, tq=tq, tk=tk, nq=nq),
        grid=(batch,),
        in_specs=[pl.BlockSpec((1, nq, w, tq), lambda b: (b, 0, 0, 0)),
                  pl.BlockSpec((seq, w), lambda b: (b, 0)),
                  pl.BlockSpec((seq, LANES), lambda b: (b, 0)),
                  pl.BlockSpec((1, seq // tk, w, tk), lambda b: (b, 0, 0, 0)),
                  pl.BlockSpec((1, HEADS, seq), lambda b: (b, 0, 0))],
        out_specs=pl.BlockSpec((seq, w), lambda b: (b, 0)),
        out_shape=jax.ShapeDtypeStruct(k.shape, BF16),
        scratch_shapes=[pltpu.VMEM((2, HEADS, 2 * LANES, tq), BF16),
                        pltpu.VMEM((2, HEADS * SUBLANES, tq), F32),
                        pltpu.VMEM((2, HEADS, ACC_ROWS, tq), F32)],
        compiler_params=_params(("parallel",)),
        name="fox_attention",
    )(qt, k, ka, vt, fqt)


def _mix_kernel(a_ref, u_ref, up_ref, z_ref, zp_ref, cb_ref, gate_ref, x_ref,
                wpa_ref, pw_ref, ps_ref, wpp_ref, cw_ref, wpc_ref, wo_ref,
                o_ref, ue_ref, ze_ref, *, tm, tiles_per_seq):
    i = pl.program_id(0) % tiles_per_seq
    has_prev = i != 0
    pad, first = SUBLANES, SUBLANES + HALO

    u = u_ref[...].astype(F32)
    ue_ref[0:pad, :] = jnp.zeros((pad, BRANCH_W), F32)
    ue_ref[pad:first, :] = jnp.where(has_prev, up_ref[...].astype(F32), 0.0)
    ue_ref[first:, :] = u
    ze_ref[0:HALO, :] = jnp.where(has_prev, zp_ref[...].astype(F32), 0.0)
    ze_ref[HALO:, :] = z_ref[...].astype(F32)

    y_a = jnp.dot(a_ref[...], wpa_ref[...], preferred_element_type=F32)

    for level in range(len(POOL_WINDOWS) - 1):
        lag = 2 ** level
        cs = slice(level * POOL_GROUP_DIM, BRANCH_W)
        ue_ref[pad:, cs] = ue_ref[pad:, cs] + ue_ref[pad - lag:pad - lag + tm + HALO, cs]
    pos = i * tm + lax.broadcasted_iota(jnp.int32, (tm, 1), 0)
    ys = []
    for gi, w in enumerate(POOL_WINDOWS):
        cs = slice(gi * POOL_GROUP_DIM, (gi + 1) * POOL_GROUP_DIM)
        win = ue_ref[first:, cs]
        if gi == len(POOL_WINDOWS) - 1:
            win = win + ue_ref[first - w // 2:first - w // 2 + tm, cs]
        n_avail = jnp.minimum(pos + 1, w).astype(F32)
        d = win / n_avail - u[:, cs]
        y = jnp.dot(d.astype(BF16), pw_ref[gi], preferred_element_type=F32)
        ys.append((y * ps_ref[:, cs]).astype(BF16))
    y_b = jnp.dot(jnp.concatenate(ys, axis=1), wpp_ref[...], preferred_element_type=F32)

    conv = cw_ref[CONV_K - 1:CONV_K, :] * ze_ref[HALO:, :]
    for lag in range(1, CONV_K):
        conv = conv + cw_ref[CONV_K - 1 - lag:CONV_K - lag, :] * ze_ref[HALO - lag:HALO - lag + tm, :]
    y_c = jnp.dot((cb_ref[...].astype(F32) * conv).astype(BF16), wpc_ref[...],
                  preferred_element_type=F32)

    mixed = None
    for n, y in enumerate((y_a, y_b, y_c)):
        gate2 = gate_ref[:, n * D_MODEL:(n + 1) * D_MODEL].astype(F32)
        mixed = gate2 * y if mixed is None else mixed + gate2 * y
    o_ref[...] = x_ref[...] + jnp.dot(mixed.astype(BF16), wo_ref[...], preferred_element_type=F32)


def _mix(a, u, z, cb, gate, x2, layer, wpa, pw, ps, wpp, cw, wpc, wo, seq):
    n = x2.shape[0]
    tm = TM_MIX
    tiles_per_seq = seq // tm
    row = lambda c: pl.BlockSpec((tm, c), lambda i: (i, 0))
    prev = pl.BlockSpec((HALO, BRANCH_W), lambda i: (jnp.maximum(i * (tm // HALO) - 1, 0), 0))
    return pl.pallas_call(
        functools.partial(_mix_kernel, tm=tm, tiles_per_seq=tiles_per_seq),
        grid=(n // tm,),
        in_specs=[row(BRANCH_W), row(BRANCH_W), prev, row(BRANCH_W), prev,
                  row(BRANCH_W), row(3 * D_MODEL), row(D_MODEL)]
                 + [_layer_spec(t, layer) for t in (wpa, pw, ps, wpp, cw, wpc, wo)],
        out_specs=row(D_MODEL),
        out_shape=jax.ShapeDtypeStruct(x2.shape, F32),
        scratch_shapes=[pltpu.VMEM((SUBLANES + HALO + tm, BRANCH_W), F32),
                        pltpu.VMEM((HALO + tm, BRANCH_W), F32)],
        compiler_params=_params(("parallel",)),
        name="branch_mix",
    )(a, u, u, z, z, cb, gate, x2, wpa, pw, ps, wpp, cw, wpc, wo)


def _ffn_kernel(x_ref, nw_ref, wgu_ref, wd_ref, fn_ref, o_ref, *, final):
    rows = x_ref.shape[0] // FFN_SUBTILES
    for s in range(FFN_SUBTILES):
        rs = slice(s * rows, (s + 1) * rows)
        x = x_ref[rs, :]
        h = _rms(x, nw_ref[...]).astype(BF16)
        acc = x
        for lo, hi in FFN_CHUNKS:
            a = jnp.dot(h, wgu_ref[:, lo:hi], preferred_element_type=F32)
            b = jnp.dot(h, wgu_ref[:, FFN_HIDDEN + lo:FFN_HIDDEN + hi], preferred_element_type=F32)
            act = (a * jax.nn.sigmoid(a) * b).astype(BF16)
            acc = acc + jnp.dot(act, wd_ref[lo:hi, :], preferred_element_type=F32)
        if final:
            acc = _rms(acc, fn_ref[...])
        o_ref[rs, :] = acc


def _ffn(x2, layer, nw, wgu, wd, fn, final):
    n = x2.shape[0]
    tm = TM_FFN
    row = pl.BlockSpec((tm, D_MODEL), lambda i: (i, 0))
    return pl.pallas_call(
        functools.partial(_ffn_kernel, final=final),
        grid=(n // tm,),
        in_specs=[row] + [_layer_spec(t, layer) for t in (nw, wgu, wd)] + [_const_spec(fn.shape)],
        out_specs=row,
        out_shape=jax.ShapeDtypeStruct(x2.shape, F32),
        compiler_params=_params(("parallel",)),
        name="swiglu_ffn",
    )(x2, nw, wgu, wd, fn)


def kernel(x, attn_norm, w_in, b_forget, b_gate, w_proj_attn, pool_w, pool_scale, w_proj_pool,
           conv_w, w_proj_conv, w_out, ffn_norm, w_gate_up, w_down, final_norm):
    batch, seq, d = x.shape
    depth = w_in.shape[0]
    x2 = x.reshape(batch * seq, d)
    w_main = _prep_w_in(w_in)
    b_f = jnp.pad(b_forget, ((0, 0), (0, LANES - HEADS))).reshape(depth, 1, LANES)
    b_g = (b_gate * 0.5).reshape(depth, 1, -1)
    w_pa, w_pp, w_pc = (w.astype(BF16) for w in (w_proj_attn, w_proj_pool, w_proj_conv))
    w_o = (w_out * 0.5).astype(BF16)
    w_pool = pool_w.astype(BF16)
    w_gu = w_gate_up.astype(BF16)
    w_d = w_down.astype(BF16)
    a_norm, f_norm = attn_norm.reshape(depth, 1, d), ffn_norm.reshape(depth, 1, d)
    p_scale = pool_scale.reshape(depth, 1, -1)
    fn = final_norm.reshape(1, d)
    for l in range(depth):
        qt, k, vt, u, z, cb, gate, lf = _inproj(x2, l, a_norm, w_main, b_f, b_g, batch, seq)
        fqt, ka = _cumsum(lf, batch, seq)
        a = _attention(qt, k, ka, vt, fqt, batch, seq)
        x2 = _mix(a, u, z, cb, gate, x2, l, w_pa, w_pool, p_scale, w_pp, conv_w, w_pc, w_o, seq)
        x2 = _ffn(x2, l, f_norm, w_gu, w_d, fn, final=(l == depth - 1))
    return x2.reshape(batch, seq, d)
```

```python
import functools
import math

import jax
import jax.numpy as jnp
from jax import lax
from jax.experimental import pallas as pl
from jax.experimental.pallas import tpu as pltpu

D_MODEL = 1024
HEADS = 8
HEAD_DIM = 64
BRANCH_W = 512
MAIN_COLS = 13 * BRANCH_W
POOL_WINDOWS = (2, 4, 8, 16)
POOL_GROUP_DIM = 128
CONV_K = 3
FFN_HIDDEN = 2816
FFN_CHUNKS = ((0, 1536), (1536, 2816))
RMS_EPS = 1e-6
NEG_INF = -1e30
LOG2E = math.log2(math.e)
LANES = 128
SUBLANES = 8
HALO = 16
HEADS_PER_SLAB = LANES // HEAD_DIM
F_TERMS = 3

BF16 = jnp.bfloat16
F32 = jnp.float32

TM_PROJ = 512
TM_MIX = 1024
TM_FFN = 512
FFN_SUBTILES = 4
TQ = 512
TK = 512
CUMSUM_CHUNK = 256
QK_LOOKAHEAD = 4
ACC_ROWS = HEAD_DIM + SUBLANES
ACC_ROWS_PADDED = HEAD_DIM + 16
VMEM_LIMIT = 56 * 1024 * 1024


def _rms(x, w):
    return x * lax.rsqrt(jnp.mean(x * x, axis=-1, keepdims=True) + RMS_EPS) * w


def _const_spec(shape):
    nd = len(shape)
    return pl.BlockSpec(shape, lambda *_: (0,) * nd, pipeline_mode=pl.Buffered(1))


def _layer_spec(stacked, layer):
    rest = stacked.shape[1:]
    return pl.BlockSpec((None,) + rest, lambda *_: (layer,) + (0,) * len(rest),
                        pipeline_mode=pl.Buffered(1))


def _params(sem):
    return pltpu.CompilerParams(dimension_semantics=sem, vmem_limit_bytes=VMEM_LIMIT)


N_MAIN_CHUNKS = MAIN_COLS // BRANCH_W
FIRST_GATE_CHUNK = N_MAIN_CHUNKS - 3 * D_MODEL // BRANCH_W


def _prep_w_in_kernel(wt_ref, o_ref):
    i = pl.program_id(1)
    w = wt_ref[0].T
    scale = jnp.where(i == 0, LOG2E * HEAD_DIM ** -0.5,
                      jnp.where((i >= FIRST_GATE_CHUNK) & (i < N_MAIN_CHUNKS), 0.5, 1.0))
    col = lax.broadcasted_iota(jnp.int32, w.shape, 1)
    keep = (i < N_MAIN_CHUNKS) | (col < HEADS)
    o_ref[...] = jnp.where(keep, w * scale, 0.0).astype(BF16)


def _prep_w_in(w_in):
    depth, k, n = w_in.shape
    assert HEADS == SUBLANES and n == MAIN_COLS + HEADS
    f_chunk = 3 * BRANCH_W // SUBLANES

    def source(l, i):
        tile = jnp.where(i < N_MAIN_CHUNKS,
                         i * (BRANCH_W // SUBLANES) + jnp.where(i >= 3, 1, 0), f_chunk)
        return (l, tile * SUBLANES, 0)

    return pl.pallas_call(
        _prep_w_in_kernel,
        grid=(depth, N_MAIN_CHUNKS + 1),
        in_specs=[pl.BlockSpec((pl.Element(1), pl.Element(BRANCH_W), pl.Element(k)), source)],
        out_specs=pl.BlockSpec((None, k, BRANCH_W), lambda l, i: (l, 0, i)),
        out_shape=jax.ShapeDtypeStruct((depth, k, MAIN_COLS + BRANCH_W), BF16),
        compiler_params=_params(("parallel", "parallel")),
        name="prep_w_in",
    )(jnp.swapaxes(w_in, 1, 2))


def _inproj_kernel(x_ref, nw_ref, w_ref, bf_ref, bg_ref,
                   qt_ref, k_ref, vt_ref, u_ref, z_ref, cb_ref, gate_ref, lf_ref):
    h = _rms(x_ref[...], nw_ref[...]).astype(BF16)

    def proj(n):
        return jnp.dot(h, w_ref[:, n * BRANCH_W:(n + 1) * BRANCH_W], preferred_element_type=F32)

    f = jnp.dot(h, w_ref[:, MAIN_COLS:MAIN_COLS + LANES], preferred_element_type=F32) + bf_ref[...]
    lf_ref[...] = jnp.minimum(f, 0.0) - jnp.log1p(jnp.exp(-jnp.abs(f)))
    qt = proj(0).T.astype(BF16)
    for c in range(qt_ref.shape[1]):
        qt_ref[0, c] = qt[:, c * TQ:(c + 1) * TQ]
    vt = proj(2).T.astype(BF16)
    for c in range(vt_ref.shape[1]):
        vt_ref[0, c] = vt[:, c * TK:(c + 1) * TK]
    for n in range(3 * D_MODEL // BRANCH_W):
        cs = slice(n * BRANCH_W, (n + 1) * BRANCH_W)
        gate_ref[:, cs] = (1.0 + jnp.tanh(proj(7 + n) + bg_ref[:, cs])).astype(BF16)
    z_ref[...] = (proj(4) * proj(6)).astype(BF16)
    u_ref[...] = proj(3).astype(BF16)
    cb_ref[...] = proj(5).astype(BF16)
    k_ref[...] = proj(1).astype(BF16)


def _inproj(x2, layer, nw, w_main, b_f, b_g, batch, seq):
    n = x2.shape[0]
    tm = TM_PROJ
    tiles_per_seq = seq // tm
    row = lambda c: pl.BlockSpec((tm, c), lambda i: (i, 0))
    assert TQ == TK
    vt_spec = pl.BlockSpec((1, tm // TK, BRANCH_W, TK),
                           lambda i: (i // tiles_per_seq, i % tiles_per_seq, 0, 0))
    act = jax.ShapeDtypeStruct((n, BRANCH_W), BF16)
    act_t = jax.ShapeDtypeStruct((batch, seq // TK, BRANCH_W, TK), BF16)
    outs = [act_t, act, act_t, act, act, act,
            jax.ShapeDtypeStruct((n, 3 * D_MODEL), BF16), jax.ShapeDtypeStruct((n, LANES), F32)]
    return pl.pallas_call(
        _inproj_kernel,
        grid=(n // tm,),
        in_specs=[row(D_MODEL)] + [_layer_spec(t, layer) for t in (nw, w_main, b_f, b_g)],
        out_specs=[vt_spec, row(BRANCH_W), vt_spec] + [row(BRANCH_W)] * 3
                  + [row(3 * D_MODEL), row(LANES)],
        out_shape=outs,
        compiler_params=_params(("parallel",)),
        name="inproj",
    )(x2, nw, w_main, b_f, b_g)


def _cumsum_kernel(lf_ref, fqt_ref, ka_ref):
    c = CUMSUM_CHUNK
    r = lax.broadcasted_iota(jnp.int32, (c, c), 0)
    s = lax.broadcasted_iota(jnp.int32, (c, c), 1)
    tri = (s <= r).astype(BF16)

    src = lax.broadcasted_iota(jnp.int32, (F_TERMS * LANES, LANES), 0)
    dst = lax.broadcasted_iota(jnp.int32, (F_TERMS * LANES, LANES), 1)
    head, term = src % LANES, src // LANES
    place = ((head < HEADS) & (dst == F_TERMS * head + term)).astype(BF16)

    def split(x):
        hi = x.astype(BF16)
        r1 = x - hi.astype(F32)
        mid = r1.astype(BF16)
        lo = (r1 - mid.astype(F32)).astype(BF16)
        return jnp.concatenate([hi, mid, lo], axis=1)

    carry = jnp.zeros((1, LANES), F32)
    for n in range(lf_ref.shape[0] // c):
        rows = slice(n * c, (n + 1) * c)
        parts = jnp.dot(tri, split(lf_ref[rows, :]), preferred_element_type=F32)
        cum = parts[:, :LANES] + parts[:, LANES:2 * LANES] + parts[:, 2 * LANES:] + carry
        carry = cum[c - 1:c, :]
        f2 = cum * LOG2E
        fqt_ref[0, :, rows] = f2.T[:HEADS, :]
        ka_ref[rows, :] = jnp.dot(split(-f2), place, preferred_element_type=F32).astype(BF16)


def _cumsum(lf, batch, seq):
    return pl.pallas_call(
        _cumsum_kernel,
        grid=(batch,),
        in_specs=[pl.BlockSpec((seq, LANES), lambda b: (b, 0))],
        out_specs=[pl.BlockSpec((1, HEADS, seq), lambda b: (b, 0, 0)),
                   pl.BlockSpec((seq, LANES), lambda b: (b, 0))],
        out_shape=[jax.ShapeDtypeStruct((batch, HEADS, seq), F32),
                   jax.ShapeDtypeStruct((batch * seq, LANES), BF16)],
        compiler_params=_params(("parallel",)),
        name="forget_cumsum",
    )(lf)


def _sublane_allreduce(x, op):
    shift = SUBLANES // 2
    while shift:
        x = op(x, pltpu.roll(x, shift, axis=0))
        shift //= 2
    return x


def _attn_kernel(qt_ref, k_ref, ka_ref, vt_ref, fq_ref, o_ref,
                 qa_ref, m_ref, acc_ref, *, tq, tk, nq):
    chan = lax.broadcasted_iota(jnp.int32, (LANES, 1), 0)
    ones_rows = jnp.ones((ACC_ROWS_PADDED - HEAD_DIM, tk), BF16)

    def begin_block(qi):
        for h in range(HEADS):
            slab, e = divmod(h, HEADS_PER_SLAB)
            q2 = qt_ref[0, qi, slab * LANES:(slab + 1) * LANES, :]
            qm = jnp.where(chan // HEAD_DIM == e, q2, jnp.zeros_like(q2))
            ones = ((chan >= F_TERMS * h) & (chan < F_TERMS * (h + 1))).astype(BF16)
            qa_ref[qi % 2, h] = jnp.concatenate([qm, jnp.broadcast_to(ones, (LANES, tq))], axis=0)
        m_ref[qi % 2] = jnp.full(m_ref.shape[1:], NEG_INF, F32)
        acc_ref[qi % 2] = jnp.zeros(acc_ref.shape[1:], F32)

    def logits_t(qi, j, h, nk, c0, nc):
        rows = slice(j * tk, j * tk + nk)
        cols = slice(h // HEADS_PER_SLAB * LANES, (h // HEADS_PER_SLAB + 1) * LANES)
        lhs = jnp.concatenate([k_ref[rows, cols], ka_ref[rows, :]], axis=1)
        return jnp.dot(lhs, qa_ref[qi % 2, h, :, c0:c0 + nc],
                       preferred_element_type=F32)

    def softmax_pv(qi, j, h, nk, c0, nc, st):
        if j == qi:
            key_pos = lax.broadcasted_iota(jnp.int32, (nk, nc), 0)
            query_pos = c0 + lax.broadcasted_iota(jnp.int32, (nk, nc), 1)
            st = jnp.where(key_pos <= query_pos, st, NEG_INF)
        st3 = st.reshape(nk // SUBLANES, SUBLANES, nc)
        stat = slice(h * SUBLANES, (h + 1) * SUBLANES)
        qc = slice(c0, c0 + nc)
        fq = jnp.broadcast_to(fq_ref[0, h:h + 1, qi * tq + c0:qi * tq + c0 + nc], (SUBLANES, nc))
        m_old = m_ref[qi % 2, stat, qc]
        m_new = jnp.maximum(m_old, _sublane_allreduce(jnp.max(st3, axis=0), jnp.maximum) + fq)
        p = jnp.exp2(st3 - (m_new - fq)[None]).reshape(nk, nc).astype(BF16)
        alpha = jnp.exp2(m_old - m_new)
        m_ref[qi % 2, stat, qc] = m_new
        v_ones = jnp.concatenate([vt_ref[0, j, h * HEAD_DIM:(h + 1) * HEAD_DIM, :nk],
                                  ones_rows[:, :nk]], axis=0)
        pv = jnp.dot(v_ones, p, preferred_element_type=F32)[:ACC_ROWS]
        acc3 = acc_ref[qi % 2, h, :, qc].reshape(ACC_ROWS // SUBLANES, SUBLANES, nc)
        acc_ref[qi % 2, h, :, qc] = (acc3 * alpha[None]).reshape(ACC_ROWS, nc) + pv

    def end_block(qi):
        outs = []
        for h in range(HEADS):
            acc3 = acc_ref[qi % 2, h].reshape(ACC_ROWS // SUBLANES, SUBLANES, tq)
            outs.append((acc3[:HEAD_DIM // SUBLANES] / acc3[HEAD_DIM // SUBLANES][None])
                        .reshape(HEAD_DIM, tq))
        o_ref[qi * tq:(qi + 1) * tq, :] = jnp.concatenate(outs, axis=0).T.astype(BF16)

    items = []
    for qi in range(nq):
        for j in range(qi + 1):
            tiles = [(tk, 0, tq)] if j < qi else [(tk // 2, 0, tq // 2), (tk, tq // 2, tq // 2)]
            items += [(qi, j, h) + tile for h in range(HEADS) for tile in tiles]
    begun, pending = set(), []

    def issue(n):
        qi = items[n][0]
        if qi not in begun:
            begun.add(qi)
            begin_block(qi)
        pending.append(logits_t(*items[n]))

    for n in range(min(QK_LOOKAHEAD, len(items))):
        issue(n)
    for n, item in enumerate(items):
        if n + QK_LOOKAHEAD < len(items):
            issue(n + QK_LOOKAHEAD)
        softmax_pv(*item, pending.pop(0))
        if n + 1 == len(items) or items[n + 1][0] != item[0]:
            end_block(item[0])


def _attention(qt, k, ka, vt, fqt, batch, seq):
    tq, tk = TQ, TK
    assert tq == tk
    nq = seq // tq
    w = HEADS * HEAD_DIM
    return pl.pallas_call(
        functools.partial(_attn_kernel, tq=tq, tk=tk, nq=nq),
        grid=(batch,),
        in_specs=[pl.BlockSpec((1, nq, w, tq), lambda b: (b, 0, 0, 0)),
                  pl.BlockSpec((seq, w), lambda b: (b, 0)),
                  pl.BlockSpec((seq, LANES), lambda b: (b, 0)),
                  pl.BlockSpec((1, seq // tk, w, tk), lambda b: (b, 0, 0, 0)),
                  pl.BlockSpec((1, HEADS, seq), lambda b: (b, 0, 0))],
        out_specs=pl.BlockSpec((seq, w), lambda b: (b, 0)),
        out_shape=jax.ShapeDtypeStruct(k.shape, BF16),
        scratch_shapes=[pltpu.VMEM((2, HEADS, 2 * LANES, tq), BF16),
                        pltpu.VMEM((2, HEADS * SUBLANES, tq), F32),
                        pltpu.VMEM((2, HEADS, ACC_ROWS, tq), F32)],
        compiler_params=_params(("parallel",)),
        name="fox_attention",
    )(qt, k, ka, vt, fqt)


def _mix_kernel(a_ref, u_ref, up_ref, z_ref, zp_ref, cb_ref, gate_ref, x_ref,
                wpa_ref, pw_ref, ps_ref, wpp_ref, cw_ref, wpc_ref, wo_ref,
                o_ref, ue_ref, ze_ref, *, tm, tiles_per_seq):
    i = pl.program_id(0) % tiles_per_seq
    has_prev = i != 0
    pad, first = SUBLANES, SUBLANES + HALO

    u = u_ref[...].astype(F32)
    ue_ref[0:pad, :] = jnp.zeros((pad, BRANCH_W), F32)
    ue_ref[pad:first, :] = jnp.where(has_prev, up_ref[...].astype(F32), 0.0)
    ue_ref[first:, :] = u
    ze_ref[0:HALO, :] = jnp.where(has_prev, zp_ref[...].astype(F32), 0.0)
    ze_ref[HALO:, :] = z_ref[...].astype(F32)

    y_a = jnp.dot(a_ref[...], wpa_ref[...], preferred_element_type=F32)

    for level in range(len(POOL_WINDOWS) - 1):
        lag = 2 ** level
        cs = slice(level * POOL_GROUP_DIM, BRANCH_W)
        ue_ref[pad:, cs] = ue_ref[pad:, cs] + ue_ref[pad - lag:pad - lag + tm + HALO, cs]
    pos = i * tm + lax.broadcasted_iota(jnp.int32, (tm, 1), 0)
    ys = []
    for gi, w in enumerate(POOL_WINDOWS):
        cs = slice(gi * POOL_GROUP_DIM, (gi + 1) * POOL_GROUP_DIM)
        win = ue_ref[first:, cs]
        if gi == len(POOL_WINDOWS) - 1:
            win = win + ue_ref[first - w // 2:first - w // 2 + tm, cs]
        n_avail = jnp.minimum(pos + 1, w).astype(F32)
        d = win / n_avail - u[:, cs]
        y = jnp.dot(d.astype(BF16), pw_ref[gi], preferred_element_type=F32)
        ys.append((y * ps_ref[:, cs]).astype(BF16))
    y_b = jnp.dot(jnp.concatenate(ys, axis=1), wpp_ref[...], preferred_element_type=F32)

    conv = cw_ref[CONV_K - 1:CONV_K, :] * ze_ref[HALO:, :]
    for lag in range(1, CONV_K):
        conv = conv + cw_ref[CONV_K - 1 - lag:CONV_K - lag, :] * ze_ref[HALO - lag:HALO - lag + tm, :]
    y_c = jnp.dot((cb_ref[...].astype(F32) * conv).astype(BF16), wpc_ref[...],
                  preferred_element_type=F32)

    mixed = None
    for n, y in enumerate((y_a, y_b, y_c)):
        gate2 = gate_ref[:, n * D_MODEL:(n + 1) * D_MODEL].astype(F32)
        mixed = gate2 * y if mixed is None else mixed + gate2 * y
    o_ref[...] = x_ref[...] + jnp.dot(mixed.astype(BF16), wo_ref[...], preferred_element_type=F32)


def _mix(a, u, z, cb, gate, x2, layer, wpa, pw, ps, wpp, cw, wpc, wo, seq):
    n = x2.shape[0]
    tm = TM_MIX
    tiles_per_seq = seq // tm
    row = lambda c: pl.BlockSpec((tm, c), lambda i: (i, 0))
    prev = pl.BlockSpec((HALO, BRANCH_W), lambda i: (jnp.maximum(i * (tm // HALO) - 1, 0), 0))
    return pl.pallas_call(
        functools.partial(_mix_kernel, tm=tm, tiles_per_seq=tiles_per_seq),
        grid=(n // tm,),
        in_specs=[row(BRANCH_W), row(BRANCH_W), prev, row(BRANCH_W), prev,
                  row(BRANCH_W), row(3 * D_MODEL), row(D_MODEL)]
                 + [_layer_spec(t, layer) for t in (wpa, pw, ps, wpp, cw, wpc, wo)],
        out_specs=row(D_MODEL),
        out_shape=jax.ShapeDtypeStruct(x2.shape, F32),
        scratch_shapes=[pltpu.VMEM((SUBLANES + HALO + tm, BRANCH_W), F32),
                        pltpu.VMEM((HALO + tm, BRANCH_W), F32)],
        compiler_params=_params(("parallel",)),
        name="branch_mix",
    )(a, u, u, z, z, cb, gate, x2, wpa, pw, ps, wpp, cw, wpc, wo)


def _ffn_kernel(x_ref, nw_ref, wgu_ref, wd_ref, fn_ref, o_ref, *, final):
    rows = x_ref.shape[0] // FFN_SUBTILES
    for s in range(FFN_SUBTILES):
        rs = slice(s * rows, (s + 1) * rows)
        x = x_ref[rs, :]
        h = _rms(x, nw_ref[...]).astype(BF16)
        acc = x
        for lo, hi in FFN_CHUNKS:
            a = jnp.dot(h, wgu_ref[:, lo:hi], preferred_element_type=F32)
            b = jnp.dot(h, wgu_ref[:, FFN_HIDDEN + lo:FFN_HIDDEN + hi], preferred_element_type=F32)
            act = (a * jax.nn.sigmoid(a) * b).astype(BF16)
            acc = acc + jnp.dot(act, wd_ref[lo:hi, :], preferred_element_type=F32)
        if final:
            acc = _rms(acc, fn_ref[...])
        o_ref[rs, :] = acc


def _ffn(x2, layer, nw, wgu, wd, fn, final):
    n = x2.shape[0]
    tm = TM_FFN
    row = pl.BlockSpec((tm, D_MODEL), lambda i: (i, 0))
    return pl.pallas_call(
        functools.partial(_ffn_kernel, final=final),
        grid=(n // tm,),
        in_specs=[row] + [_layer_spec(t, layer) for t in (nw, wgu, wd)] + [_const_spec(fn.shape)],
        out_specs=row,
        out_shape=jax.ShapeDtypeStruct(x2.shape, F32),
        compiler_params=_params(("parallel",)),
        name="swiglu_ffn",
    )(x2, nw, wgu, wd, fn)


def kernel(x, attn_norm, w_in, b_forget, b_gate, w_proj_attn, pool_w, pool_scale, w_proj_pool,
           conv_w, w_proj_conv, w_out, ffn_norm, w_gate_up, w_down, final_norm):
    batch, seq, d = x.shape
    depth = w_in.shape[0]
    x2 = x.reshape(batch * seq, d)
    w_main = _prep_w_in(w_in)
    b_f = jnp.pad(b_forget, ((0, 0), (0, LANES - HEADS))).reshape(depth, 1, LANES)
    b_g = (b_gate * 0.5).reshape(depth, 1, -1)
    w_pa, w_pp, w_pc = (w.astype(BF16) for w in (w_proj_attn, w_proj_pool, w_proj_conv))
    w_o = (w_out * 0.5).astype(BF16)
    w_pool = pool_w.astype(BF16)
    w_gu = w_gate_up.astype(BF16)
    w_d = w_down.astype(BF16)
    a_norm, f_norm = attn_norm.reshape(depth, 1, d), ffn_norm.reshape(depth, 1, d)
    p_scale = pool_scale.reshape(depth, 1, -1)
    fn = final_norm.reshape(1, d)
    for l in range(depth):
        qt, k, vt, u, z, cb, gate, lf = _inproj(x2, l, a_norm, w_main, b_f, b_g, batch, seq)
        fqt, ka = _cumsum(lf, batch, seq)
        a = _attention(qt, k, ka, vt, fqt, batch, seq)
        x2 = _mix(a, u, z, cb, gate, x2, l, w_pa, w_pool, p_scale, w_pp, conv_w, w_pc, w_o, seq)
        x2 = _ffn(x2, l, f_norm, w_gu, w_d, fn, final=(l == depth - 1))
    return x2.reshape(batch, seq, d)
```

```python
import functools
import math

import jax
import jax.numpy as jnp
from jax import lax
from jax.experimental import pallas as pl
from jax.experimental.pallas import tpu as pltpu

D_MODEL = 1024
HEADS = 8
HEAD_DIM = 64
BRANCH_W = 512
MAIN_COLS = 13 * BRANCH_W
POOL_WINDOWS = (2, 4, 8, 16)
POOL_GROUP_DIM = 128
CONV_K = 3
FFN_HIDDEN = 2816
FFN_CHUNKS = ((0, 1536), (1536, 2816))
RMS_EPS = 1e-6
NEG_INF = -1e30
LOG2E = math.log2(math.e)
LANES = 128
SUBLANES = 8
HALO = 16
HEADS_PER_SLAB = LANES // HEAD_DIM
F_TERMS = 3

BF16 = jnp.bfloat16
F32 = jnp.float32

TM_PROJ = 512
TM_MIX = 1024
TM_FFN = 512
FFN_SUBTILES = 4
TQ = 512
TK = 512
CUMSUM_CHUNK = 256
QK_LOOKAHEAD = 6
ACC_ROWS = HEAD_DIM + SUBLANES
ACC_ROWS_PADDED = HEAD_DIM + 16
VMEM_LIMIT = 56 * 1024 * 1024


def _rms(x, w):
    return x * lax.rsqrt(jnp.mean(x * x, axis=-1, keepdims=True) + RMS_EPS) * w


def _const_spec(shape):
    nd = len(shape)
    return pl.BlockSpec(shape, lambda *_: (0,) * nd, pipeline_mode=pl.Buffered(1))


def _layer_spec(stacked, layer):
    rest = stacked.shape[1:]
    return pl.BlockSpec((None,) + rest, lambda *_: (layer,) + (0,) * len(rest),
                        pipeline_mode=pl.Buffered(1))


def _params(sem):
    return pltpu.CompilerParams(dimension_semantics=sem, vmem_limit_bytes=VMEM_LIMIT)


N_MAIN_CHUNKS = MAIN_COLS // BRANCH_W
FIRST_GATE_CHUNK = N_MAIN_CHUNKS - 3 * D_MODEL // BRANCH_W


def _prep_w_in_kernel(wt_ref, o_ref):
    i = pl.program_id(1)
    w = wt_ref[0].T
    scale = jnp.where(i == 0, LOG2E * HEAD_DIM ** -0.5,
                      jnp.where((i >= FIRST_GATE_CHUNK) & (i < N_MAIN_CHUNKS), 0.5, 1.0))
    col = lax.broadcasted_iota(jnp.int32, w.shape, 1)
    keep = (i < N_MAIN_CHUNKS) | (col < HEADS)
    o_ref[...] = jnp.where(keep, w * scale, 0.0).astype(BF16)


def _prep_w_in(w_in):
    depth, k, n = w_in.shape
    assert HEADS == SUBLANES and n == MAIN_COLS + HEADS
    f_chunk = 3 * BRANCH_W // SUBLANES

    def source(l, i):
        tile = jnp.where(i < N_MAIN_CHUNKS,
                         i * (BRANCH_W // SUBLANES) + jnp.where(i >= 3, 1, 0), f_chunk)
        return (l, tile * SUBLANES, 0)

    return pl.pallas_call(
        _prep_w_in_kernel,
        grid=(depth, N_MAIN_CHUNKS + 1),
        in_specs=[pl.BlockSpec((pl.Element(1), pl.Element(BRANCH_W), pl.Element(k)), source)],
        out_specs=pl.BlockSpec((None, k, BRANCH_W), lambda l, i: (l, 0, i)),
        out_shape=jax.ShapeDtypeStruct((depth, k, MAIN_COLS + BRANCH_W), BF16),
        compiler_params=_params(("parallel", "parallel")),
        name="prep_w_in",
    )(jnp.swapaxes(w_in, 1, 2))


def _inproj_kernel(x_ref, nw_ref, w_ref, bf_ref, bg_ref,
                   qt_ref, k_ref, vt_ref, u_ref, z_ref, cb_ref, gate_ref, lf_ref):
    h = _rms(x_ref[...], nw_ref[...]).astype(BF16)

    def proj(n):
        return jnp.dot(h, w_ref[:, n * BRANCH_W:(n + 1) * BRANCH_W], preferred_element_type=F32)

    f = jnp.dot(h, w_ref[:, MAIN_COLS:MAIN_COLS + LANES], preferred_element_type=F32) + bf_ref[...]
    lf_ref[...] = jnp.minimum(f, 0.0) - jnp.log1p(jnp.exp(-jnp.abs(f)))
    qt = proj(0).T.astype(BF16)
    for c in range(qt_ref.shape[1]):
        qt_ref[0, c] = qt[:, c * TQ:(c + 1) * TQ]
    vt = proj(2).T.astype(BF16)
    for c in range(vt_ref.shape[1]):
        vt_ref[0, c] = vt[:, c * TK:(c + 1) * TK]
    for n in range(3 * D_MODEL // BRANCH_W):
        cs = slice(n * BRANCH_W, (n + 1) * BRANCH_W)
        gate_ref[:, cs] = (1.0 + jnp.tanh(proj(7 + n) + bg_ref[:, cs])).astype(BF16)
    z_ref[...] = (proj(4) * proj(6)).astype(BF16)
    u_ref[...] = proj(3).astype(BF16)
    cb_ref[...] = proj(5).astype(BF16)
    k_ref[...] = proj(1).astype(BF16)


def _inproj(x2, layer, nw, w_main, b_f, b_g, batch, seq):
    n = x2.shape[0]
    tm = TM_PROJ
    tiles_per_seq = seq // tm
    row = lambda c: pl.BlockSpec((tm, c), lambda i: (i, 0))
    assert TQ == TK
    vt_spec = pl.BlockSpec((1, tm // TK, BRANCH_W, TK),
                           lambda i: (i // tiles_per_seq, i % tiles_per_seq, 0, 0))
    act = jax.ShapeDtypeStruct((n, BRANCH_W), BF16)
    act_t = jax.ShapeDtypeStruct((batch, seq // TK, BRANCH_W, TK), BF16)
    outs = [act_t, act, act_t, act, act, act,
            jax.ShapeDtypeStruct((n, 3 * D_MODEL), BF16), jax.ShapeDtypeStruct((n, LANES), F32)]
    return pl.pallas_call(
        _inproj_kernel,
        grid=(n // tm,),
        in_specs=[row(D_MODEL)] + [_layer_spec(t, layer) for t in (nw, w_main, b_f, b_g)],
        out_specs=[vt_spec, row(BRANCH_W), vt_spec] + [row(BRANCH_W)] * 3
                  + [row(3 * D_MODEL), row(LANES)],
        out_shape=outs,
        compiler_params=_params(("parallel",)),
        name="inproj",
    )(x2, nw, w_main, b_f, b_g)


def _cumsum_kernel(lf_ref, fqt_ref, ka_ref):
    c = CUMSUM_CHUNK
    r = lax.broadcasted_iota(jnp.int32, (c, c), 0)
    s = lax.broadcasted_iota(jnp.int32, (c, c), 1)
    tri = (s <= r).astype(BF16)

    src = lax.broadcasted_iota(jnp.int32, (F_TERMS * LANES, LANES), 0)
    dst = lax.broadcasted_iota(jnp.int32, (F_TERMS * LANES, LANES), 1)
    head, term = src % LANES, src // LANES
    place = ((head < HEADS) & (dst == F_TERMS * head + term)).astype(BF16)

    def split(x):
        hi = x.astype(BF16)
        r1 = x - hi.astype(F32)
        mid = r1.astype(BF16)
        lo = (r1 - mid.astype(F32)).astype(BF16)
        return jnp.concatenate([hi, mid, lo], axis=1)

    carry = jnp.zeros((1, LANES), F32)
    for n in range(lf_ref.shape[0] // c):
        rows = slice(n * c, (n + 1) * c)
        parts = jnp.dot(tri, split(lf_ref[rows, :]), preferred_element_type=F32)
        cum = parts[:, :LANES] + parts[:, LANES:2 * LANES] + parts[:, 2 * LANES:] + carry
        carry = cum[c - 1:c, :]
        f2 = cum * LOG2E
        fqt_ref[0, :, rows] = f2.T[:HEADS, :]
        ka_ref[rows, :] = jnp.dot(split(-f2), place, preferred_element_type=F32).astype(BF16)


def _cumsum(lf, batch, seq):
    return pl.pallas_call(
        _cumsum_kernel,
        grid=(batch,),
        in_specs=[pl.BlockSpec((seq, LANES), lambda b: (b, 0))],
        out_specs=[pl.BlockSpec((1, HEADS, seq), lambda b: (b, 0, 0)),
                   pl.BlockSpec((seq, LANES), lambda b: (b, 0))],
        out_shape=[jax.ShapeDtypeStruct((batch, HEADS, seq), F32),
                   jax.ShapeDtypeStruct((batch * seq, LANES), BF16)],
        compiler_params=_params(("parallel",)),
        name="forget_cumsum",
    )(lf)


def _sublane_allreduce(x, op):
    shift = SUBLANES // 2
    while shift:
        x = op(x, pltpu.roll(x, shift, axis=0))
        shift //= 2
    return x


def _attn_kernel(qt_ref, k_ref, ka_ref, vt_ref, fq_ref, o_ref,
                 qa_ref, m_ref, acc_ref, *, tq, tk, nq):
    chan = lax.broadcasted_iota(jnp.int32, (LANES, 1), 0)
    ones_rows = jnp.ones((ACC_ROWS_PADDED - HEAD_DIM, tk), BF16)

    def begin_block(qi):
        for h in range(HEADS):
            slab, e = divmod(h, HEADS_PER_SLAB)
            q2 = qt_ref[0, qi, slab * LANES:(slab + 1) * LANES, :]
            qm = jnp.where(chan // HEAD_DIM == e, q2, jnp.zeros_like(q2))
            ones = ((chan >= F_TERMS * h) & (chan < F_TERMS * (h + 1))).astype(BF16)
            qa_ref[qi % 2, h] = jnp.concatenate([qm, jnp.broadcast_to(ones, (LANES, tq))], axis=0)
        m_ref[qi % 2] = jnp.full(m_ref.shape[1:], NEG_INF, F32)
        acc_ref[qi % 2] = jnp.zeros(acc_ref.shape[1:], F32)

    def logits_t(qi, j, h, nk, c0, nc):
        rows = slice(j * tk, j * tk + nk)
        cols = slice(h // HEADS_PER_SLAB * LANES, (h // HEADS_PER_SLAB + 1) * LANES)
        lhs = jnp.concatenate([k_ref[rows, cols], ka_ref[rows, :]], axis=1)
        return jnp.dot(lhs, qa_ref[qi % 2, h, :, c0:c0 + nc],
                       preferred_element_type=F32)

    def softmax_pv(qi, j, h, nk, c0, nc, st):
        if j == qi:
            key_pos = lax.broadcasted_iota(jnp.int32, (nk, nc), 0)
            query_pos = c0 + lax.broadcasted_iota(jnp.int32, (nk, nc), 1)
            st = jnp.where(key_pos <= query_pos, st, NEG_INF)
        st3 = st.reshape(nk // SUBLANES, SUBLANES, nc)
        stat = slice(h * SUBLANES, (h + 1) * SUBLANES)
        qc = slice(c0, c0 + nc)
        fq = jnp.broadcast_to(fq_ref[0, h:h + 1, qi * tq + c0:qi * tq + c0 + nc], (SUBLANES, nc))
        m_old = m_ref[qi % 2, stat, qc]
        m_new = jnp.maximum(m_old, _sublane_allreduce(jnp.max(st3, axis=0), jnp.maximum) + fq)
        p = jnp.exp2(st3 - (m_new - fq)[None]).reshape(nk, nc).astype(BF16)
        alpha = jnp.exp2(m_old - m_new)
        m_ref[qi % 2, stat, qc] = m_new
        v_ones = jnp.concatenate([vt_ref[0, j, h * HEAD_DIM:(h + 1) * HEAD_DIM, :nk],
                                  ones_rows[:, :nk]], axis=0)
        pv = jnp.dot(v_ones, p, preferred_element_type=F32)[:ACC_ROWS]
        acc3 = acc_ref[qi % 2, h, :, qc].reshape(ACC_ROWS // SUBLANES, SUBLANES, nc)
        acc_ref[qi % 2, h, :, qc] = (acc3 * alpha[None]).reshape(ACC_ROWS, nc) + pv

    def end_block(qi):
        outs = []
        for h in range(HEADS):
            acc3 = acc_ref[qi % 2, h].reshape(ACC_ROWS // SUBLANES, SUBLANES, tq)
            outs.append((acc3[:HEAD_DIM // SUBLANES] / acc3[HEAD_DIM // SUBLANES][None])
                        .reshape(HEAD_DIM, tq))
        o_ref[qi * tq:(qi + 1) * tq, :] = jnp.concatenate(outs, axis=0).T.astype(BF16)

    items = []
    for qi in range(nq):
        for j in range(qi + 1):
            tiles = [(tk, 0, tq)] if j < qi else [(tk // 2, 0, tq // 2), (tk, tq // 2, tq // 2)]
            items += [(qi, j, h) + tile for h in range(HEADS) for tile in tiles]
    begun, pending = set(), []

    def issue(n):
        qi = items[n][0]
        if qi not in begun:
            begun.add(qi)
            begin_block(qi)
        pending.append(logits_t(*items[n]))

    for n in range(min(QK_LOOKAHEAD, len(items))):
        issue(n)
    for n, item in enumerate(items):
        if n + QK_LOOKAHEAD < len(items):
            issue(n + QK_LOOKAHEAD)
        softmax_pv(*item, pending.pop(0))
        if n + 1 == len(items) or items[n + 1][0] != item[0]:
            end_block(item[0])


def _attention(qt, k, ka, vt, fqt, batch, seq):
    tq, tk = TQ, TK
    assert tq == tk
    nq = seq // tq
    w = HEADS * HEAD_DIM
    return pl.pallas_call(
        functools.partial(_attn_kernel, tq=tq, tk=tk, nq=nq),
        grid=(batch,),
        in_specs=[pl.BlockSpec((1, nq, w, tq), lambda b: (b, 0, 0, 0)),
                  pl.BlockSpec((seq, w), lambda b: (b, 0)),
                  pl.BlockSpec((seq, LANES), lambda b: (b, 0)),
                  pl.BlockSpec((1, seq // tk, w, tk), lambda b: (b, 0, 0, 0)),
                  pl.BlockSpec((1, HEADS, seq), lambda b: (b, 0, 0))],
        out_specs=pl.BlockSpec((seq, w), lambda b: (b, 0)),
        out_shape=jax.ShapeDtypeStruct(k.shape, BF16),
        scratch_shapes=[pltpu.VMEM((2, HEADS, 2 * LANES, tq), BF16),
                        pltpu.VMEM((2, HEADS * SUBLANES, tq), F32),
                        pltpu.VMEM((2, HEADS, ACC_ROWS, tq), F32)],
        compiler_params=_params(("parallel",)),
        name="fox_attention",
    )(qt, k, ka, vt, fqt)


def _mix_kernel(a_ref, u_ref, up_ref, z_ref, zp_ref, cb_ref, gate_ref, x_ref,
                wpa_ref, pw_ref, ps_ref, wpp_ref, cw_ref, wpc_ref, wo_ref,
                o_ref, ue_ref, ze_ref, *, tm, tiles_per_seq):
    i = pl.program_id(0) % tiles_per_seq
    has_prev = i != 0
    pad, first = SUBLANES, SUBLANES + HALO

    u = u_ref[...].astype(F32)
    ue_ref[0:pad, :] = jnp.zeros((pad, BRANCH_W), F32)
    ue_ref[pad:first, :] = jnp.where(has_prev, up_ref[...].astype(F32), 0.0)
    ue_ref[first:, :] = u
    ze_ref[0:HALO, :] = jnp.where(has_prev, zp_ref[...].astype(F32), 0.0)
    ze_ref[HALO:, :] = z_ref[...].astype(F32)

    y_a = jnp.dot(a_ref[...], wpa_ref[...], preferred_element_type=F32)

    for level in range(len(POOL_WINDOWS) - 1):
        lag = 2 ** level
        cs = slice(level * POOL_GROUP_DIM, BRANCH_W)
        ue_ref[pad:, cs] = ue_ref[pad:, cs] + ue_ref[pad - lag:pad - lag + tm + HALO, cs]
    pos = i * tm + lax.broadcasted_iota(jnp.int32, (tm, 1), 0)
    ys = []
    for gi, w in enumerate(POOL_WINDOWS):
        cs = slice(gi * POOL_GROUP_DIM, (gi + 1) * POOL_GROUP_DIM)
        win = ue_ref[first:, cs]
        if gi == len(POOL_WINDOWS) - 1:
            win = win + ue_ref[first - w // 2:first - w // 2 + tm, cs]
        n_avail = jnp.minimum(pos + 1, w).astype(F32)
        d = win / n_avail - u[:, cs]
        y = jnp.dot(d.astype(BF16), pw_ref[gi], preferred_element_type=F32)
        ys.append((y * ps_ref[:, cs]).astype(BF16))
    y_b = jnp.dot(jnp.concatenate(ys, axis=1), wpp_ref[...], preferred_element_type=F32)

    conv = cw_ref[CONV_K - 1:CONV_K, :] * ze_ref[HALO:, :]
    for lag in range(1, CONV_K):
        conv = conv + cw_ref[CONV_K - 1 - lag:CONV_K - lag, :] * ze_ref[HALO - lag:HALO - lag + tm, :]
    y_c = jnp.dot((cb_ref[...].astype(F32) * conv).astype(BF16), wpc_ref[...],
                  preferred_element_type=F32)

    mixed = None
    for n, y in enumerate((y_a, y_b, y_c)):
        gate2 = gate_ref[:, n * D_MODEL:(n + 1) * D_MODEL].astype(F32)
        mixed = gate2 * y if mixed is None else mixed + gate2 * y
    o_ref[...] = x_ref[...] + jnp.dot(mixed.astype(BF16), wo_ref[...], preferred_element_type=F32)


def _mix(a, u, z, cb, gate, x2, layer, wpa, pw, ps, wpp, cw, wpc, wo, seq):
    n = x2.shape[0]
    tm = TM_MIX
    tiles_per_seq = seq // tm
    row = lambda c: pl.BlockSpec((tm, c), lambda i: (i, 0))
    prev = pl.BlockSpec((HALO, BRANCH_W), lambda i: (jnp.maximum(i * (tm // HALO) - 1, 0), 0))
    return pl.pallas_call(
        functools.partial(_mix_kernel, tm=tm, tiles_per_seq=tiles_per_seq),
        grid=(n // tm,),
        in_specs=[row(BRANCH_W), row(BRANCH_W), prev, row(BRANCH_W), prev,
                  row(BRANCH_W), row(3 * D_MODEL), row(D_MODEL)]
                 + [_layer_spec(t, layer) for t in (wpa, pw, ps, wpp, cw, wpc, wo)],
        out_specs=row(D_MODEL),
        out_shape=jax.ShapeDtypeStruct(x2.shape, F32),
        scratch_shapes=[pltpu.VMEM((SUBLANES + HALO + tm, BRANCH_W), F32),
                        pltpu.VMEM((HALO + tm, BRANCH_W), F32)],
        compiler_params=_params(("parallel",)),
        name="branch_mix",
    )(a, u, u, z, z, cb, gate, x2, wpa, pw, ps, wpp, cw, wpc, wo)


def _ffn_kernel(x_ref, nw_ref, wgu_ref, wd_ref, fn_ref, o_ref, *, final):
    rows = x_ref.shape[0] // FFN_SUBTILES
    for s in range(FFN_SUBTILES):
        rs = slice(s * rows, (s + 1) * rows)
        x = x_ref[rs, :]
        h = _rms(x, nw_ref[...]).astype(BF16)
        acc = x
        for lo, hi in FFN_CHUNKS:
            a = jnp.dot(h, wgu_ref[:, lo:hi], preferred_element_type=F32)
            b = jnp.dot(h, wgu_ref[:, FFN_HIDDEN + lo:FFN_HIDDEN + hi], preferred_element_type=F32)
            act = (a * jax.nn.sigmoid(a) * b).astype(BF16)
            acc = acc + jnp.dot(act, wd_ref[lo:hi, :], preferred_element_type=F32)
        if final:
            acc = _rms(acc, fn_ref[...])
        o_ref[rs, :] = acc


def _ffn(x2, layer, nw, wgu, wd, fn, final):
    n = x2.shape[0]
    tm = TM_FFN
    row = pl.BlockSpec((tm, D_MODEL), lambda i: (i, 0))
    return pl.pallas_call(
        functools.partial(_ffn_kernel, final=final),
        grid=(n // tm,),
        in_specs=[row] + [_layer_spec(t, layer) for t in (nw, wgu, wd)] + [_const_spec(fn.shape)],
        out_specs=row,
        out_shape=jax.ShapeDtypeStruct(x2.shape, F32),
        compiler_params=_params(("parallel",)),
        name="swiglu_ffn",
    )(x2, nw, wgu, wd, fn)


def kernel(x, attn_norm, w_in, b_forget, b_gate, w_proj_attn, pool_w, pool_scale, w_proj_pool,
           conv_w, w_proj_conv, w_out, ffn_norm, w_gate_up, w_down, final_norm):
    batch, seq, d = x.shape
    depth = w_in.shape[0]
    x2 = x.reshape(batch * seq, d)
    w_main = _prep_w_in(w_in)
    b_f = jnp.pad(b_forget, ((0, 0), (0, LANES - HEADS))).reshape(depth, 1, LANES)
    b_g = (b_gate * 0.5).reshape(depth, 1, -1)
    w_pa, w_pp, w_pc = (w.astype(BF16) for w in (w_proj_attn, w_proj_pool, w_proj_conv))
    w_o = (w_out * 0.5).astype(BF16)
    w_pool = pool_w.astype(BF16)
    w_gu = w_gate_up.astype(BF16)
    w_d = w_down.astype(BF16)
    a_norm, f_norm = attn_norm.reshape(depth, 1, d), ffn_norm.reshape(depth, 1, d)
    p_scale = pool_scale.reshape(depth, 1, -1)
    fn = final_norm.reshape(1, d)
    for l in range(depth):
        qt, k, vt, u, z, cb, gate, lf = _inproj(x2, l, a_norm, w_main, b_f, b_g, batch, seq)
        fqt, ka = _cumsum(lf, batch, seq)
        a = _attention(qt, k, ka, vt, fqt, batch, seq)
        x2 = _mix(a, u, z, cb, gate, x2, l, w_pa, w_pool, p_scale, w_pp, conv_w, w_pc, w_o, seq)
        x2 = _ffn(x2, l, f_norm, w_gu, w_d, fn, final=(l == depth - 1))
    return x2.reshape(batch, seq, d)
```

```python
import functools
import math

import jax
import jax.numpy as jnp
from jax import lax
from jax.experimental import pallas as pl
from jax.experimental.pallas import tpu as pltpu

D_MODEL = 1024
HEADS = 8
HEAD_DIM = 64
BRANCH_W = 512
MAIN_COLS = 13 * BRANCH_W
POOL_WINDOWS = (2, 4, 8, 16)
POOL_GROUP_DIM = 128
CONV_K = 3
FFN_HIDDEN = 2816
FFN_CHUNKS = ((0, 1536), (1536, 2816))
RMS_EPS = 1e-6
NEG_INF = -1e30
LOG2E = math.log2(math.e)
LANES = 128
SUBLANES = 8
HALO = 16
HEADS_PER_SLAB = LANES // HEAD_DIM
F_TERMS = 3

BF16 = jnp.bfloat16
F32 = jnp.float32

TM_PROJ = 512
TM_MIX = 1024
TM_FFN = 512
FFN_SUBTILES = 4
TQ = 512
TK = 512
CUMSUM_CHUNK = 256
QK_LOOKAHEAD = 6
ACC_ROWS = HEAD_DIM + SUBLANES
ACC_ROWS_PADDED = HEAD_DIM + 16
VMEM_LIMIT = 56 * 1024 * 1024


def _rms(x, w):
    return x * lax.rsqrt(jnp.mean(x * x, axis=-1, keepdims=True) + RMS_EPS) * w


def _const_spec(shape):
    nd = len(shape)
    return pl.BlockSpec(shape, lambda *_: (0,) * nd, pipeline_mode=pl.Buffered(1))


def _layer_spec(stacked, layer):
    rest = stacked.shape[1:]
    return pl.BlockSpec((None,) + rest, lambda *_: (layer,) + (0,) * len(rest),
                        pipeline_mode=pl.Buffered(1))


def _params(sem):
    return pltpu.CompilerParams(dimension_semantics=sem, vmem_limit_bytes=VMEM_LIMIT)


N_MAIN_CHUNKS = MAIN_COLS // BRANCH_W
FIRST_GATE_CHUNK = N_MAIN_CHUNKS - 3 * D_MODEL // BRANCH_W


def _prep_w_in_kernel(wt_ref, o_ref):
    i = pl.program_id(1)
    w = wt_ref[0].T
    scale = jnp.where(i == 0, LOG2E * HEAD_DIM ** -0.5,
                      jnp.where((i >= FIRST_GATE_CHUNK) & (i < N_MAIN_CHUNKS), 0.5, 1.0))
    col = lax.broadcasted_iota(jnp.int32, w.shape, 1)
    keep = (i < N_MAIN_CHUNKS) | (col < HEADS)
    o_ref[...] = jnp.where(keep, w * scale, 0.0).astype(BF16)


def _prep_w_in(w_in):
    depth, k, n = w_in.shape
    assert HEADS == SUBLANES and n == MAIN_COLS + HEADS
    f_chunk = 3 * BRANCH_W // SUBLANES

    def source(l, i):
        tile = jnp.where(i < N_MAIN_CHUNKS,
                         i * (BRANCH_W // SUBLANES) + jnp.where(i >= 3, 1, 0), f_chunk)
        return (l, tile * SUBLANES, 0)

    return pl.pallas_call(
        _prep_w_in_kernel,
        grid=(depth, N_MAIN_CHUNKS + 1),
        in_specs=[pl.BlockSpec((pl.Element(1), pl.Element(BRANCH_W), pl.Element(k)), source)],
        out_specs=pl.BlockSpec((None, k, BRANCH_W), lambda l, i: (l, 0, i)),
        out_shape=jax.ShapeDtypeStruct((depth, k, MAIN_COLS + BRANCH_W), BF16),
        compiler_params=_params(("parallel", "parallel")),
        name="prep_w_in",
    )(jnp.swapaxes(w_in, 1, 2))


def _inproj_kernel(x_ref, nw_ref, w_ref, bf_ref, bg_ref,
                   qt_ref, k_ref, vt_ref, u_ref, z_ref, cb_ref, gate_ref, lf_ref):
    h = _rms(x_ref[...], nw_ref[...]).astype(BF16)

    def proj(n):
        return jnp.dot(h, w_ref[:, n * BRANCH_W:(n + 1) * BRANCH_W], preferred_element_type=F32)

    f = jnp.dot(h, w_ref[:, MAIN_COLS:MAIN_COLS + LANES], preferred_element_type=F32) + bf_ref[...]
    lf_ref[...] = jnp.minimum(f, 0.0) - jnp.log1p(jnp.exp(-jnp.abs(f)))
    qt = proj(0).T.astype(BF16)
    for c in range(qt_ref.shape[1]):
        qt_ref[0, c] = qt[:, c * TQ:(c + 1) * TQ]
    vt = proj(2).T.astype(BF16)
    for c in range(vt_ref.shape[1]):
        vt_ref[0, c] = vt[:, c * TK:(c + 1) * TK]
    for n in range(3 * D_MODEL // BRANCH_W):
        cs = slice(n * BRANCH_W, (n + 1) * BRANCH_W)
        gate_ref[:, cs] = (1.0 + jnp.tanh(proj(7 + n) + bg_ref[:, cs])).astype(BF16)
    z_ref[...] = (proj(4) * proj(6)).astype(BF16)
    u_ref[...] = proj(3).astype(BF16)
    cb_ref[...] = proj(5).astype(BF16)
    k_ref[...] = proj(1).astype(BF16)


def _inproj(x2, layer, nw, w_main, b_f, b_g, batch, seq):
    n = x2.shape[0]
    tm = TM_PROJ
    tiles_per_seq = seq // tm
    row = lambda c: pl.BlockSpec((tm, c), lambda i: (i, 0))
    assert TQ == TK
    vt_spec = pl.BlockSpec((1, tm // TK, BRANCH_W, TK),
                           lambda i: (i // tiles_per_seq, i % tiles_per_seq, 0, 0))
    act = jax.ShapeDtypeStruct((n, BRANCH_W), BF16)
    act_t = jax.ShapeDtypeStruct((batch, seq // TK, BRANCH_W, TK), BF16)
    outs = [act_t, act, act_t, act, act, act,
            jax.ShapeDtypeStruct((n, 3 * D_MODEL), BF16), jax.ShapeDtypeStruct((n, LANES), F32)]
    return pl.pallas_call(
        _inproj_kernel,
        grid=(n // tm,),
        in_specs=[row(D_MODEL)] + [_layer_spec(t, layer) for t in (nw, w_main, b_f, b_g)],
        out_specs=[vt_spec, row(BRANCH_W), vt_spec] + [row(BRANCH_W)] * 3
                  + [row(3 * D_MODEL), row(LANES)],
        out_shape=outs,
        compiler_params=_params(("parallel",)),
        name="inproj",
    )(x2, nw, w_main, b_f, b_g)


def _cumsum_kernel(lf_ref, fqt_ref, ka_ref):
    c = CUMSUM_CHUNK
    r = lax.broadcasted_iota(jnp.int32, (c, c), 0)
    s = lax.broadcasted_iota(jnp.int32, (c, c), 1)
    tri = (s <= r).astype(BF16)

    src = lax.broadcasted_iota(jnp.int32, (F_TERMS * LANES, LANES), 0)
    dst = lax.broadcasted_iota(jnp.int32, (F_TERMS * LANES, LANES), 1)
    head, term = src % LANES, src // LANES
    place = ((head < HEADS) & (dst == F_TERMS * head + term)).astype(BF16)

    def split(x):
        hi = x.astype(BF16)
        r1 = x - hi.astype(F32)
        mid = r1.astype(BF16)
        lo = (r1 - mid.astype(F32)).astype(BF16)
        return jnp.concatenate([hi, mid, lo], axis=1)

    carry = jnp.zeros((1, LANES), F32)
    for n in range(lf_ref.shape[0] // c):
        rows = slice(n * c, (n + 1) * c)
        parts = jnp.dot(tri, split(lf_ref[rows, :]), preferred_element_type=F32)
        cum = parts[:, :LANES] + parts[:, LANES:2 * LANES] + parts[:, 2 * LANES:] + carry
        carry = cum[c - 1:c, :]
        f2 = cum * LOG2E
        fqt_ref[0, :, rows] = f2.T[:HEADS, :]
        ka_ref[rows, :] = jnp.dot(split(-f2), place, preferred_element_type=F32).astype(BF16)


def _cumsum(lf, batch, seq):
    return pl.pallas_call(
        _cumsum_kernel,
        grid=(batch,),
        in_specs=[pl.BlockSpec((seq, LANES), lambda b: (b, 0))],
        out_specs=[pl.BlockSpec((1, HEADS, seq), lambda b: (b, 0, 0)),
                   pl.BlockSpec((seq, LANES), lambda b: (b, 0))],
        out_shape=[jax.ShapeDtypeStruct((batch, HEADS, seq), F32),
                   jax.ShapeDtypeStruct((batch * seq, LANES), BF16)],
        compiler_params=_params(("parallel",)),
        name="forget_cumsum",
    )(lf)


def _sublane_allreduce(x, op):
    shift = SUBLANES // 2
    while shift:
        x = op(x, pltpu.roll(x, shift, axis=0))
        shift //= 2
    return x


def _attn_kernel(qt_ref, k_ref, ka_ref, vt_ref, fq_ref, o_ref,
                 qa_ref, m_ref, acc_ref, *, tq, tk, nq):
    chan = lax.broadcasted_iota(jnp.int32, (LANES, 1), 0)
    ones_rows = jnp.ones((ACC_ROWS_PADDED - HEAD_DIM, tk), BF16)

    def begin_block(qi):
        for h in range(HEADS):
            slab, e = divmod(h, HEADS_PER_SLAB)
            q2 = qt_ref[0, qi, slab * LANES:(slab + 1) * LANES, :]
            qm = jnp.where(chan // HEAD_DIM == e, q2, jnp.zeros_like(q2))
            ones = ((chan >= F_TERMS * h) & (chan < F_TERMS * (h + 1))).astype(BF16)
            qa_ref[qi % 2, h] = jnp.concatenate([qm, jnp.broadcast_to(ones, (LANES, tq))], axis=0)
        m_ref[qi % 2] = jnp.full(m_ref.shape[1:], NEG_INF, F32)
        acc_ref[qi % 2] = jnp.zeros(acc_ref.shape[1:], F32)

    def logits_t(qi, j, h, nk, c0, nc):
        rows = slice(j * tk, j * tk + nk)
        cols = slice(h // HEADS_PER_SLAB * LANES, (h // HEADS_PER_SLAB + 1) * LANES)
        lhs = jnp.concatenate([k_ref[rows, cols], ka_ref[rows, :]], axis=1)
        return jnp.dot(lhs, qa_ref[qi % 2, h, :, c0:c0 + nc],
                       preferred_element_type=F32)

    def softmax_pv(qi, j, h, nk, c0, nc, st):
        if j == qi:
            key_pos = lax.broadcasted_iota(jnp.int32, (nk, nc), 0)
            query_pos = c0 + lax.broadcasted_iota(jnp.int32, (nk, nc), 1)
            st = jnp.where(key_pos <= query_pos, st, NEG_INF)
        st3 = st.reshape(nk // SUBLANES, SUBLANES, nc)
        stat = slice(h * SUBLANES, (h + 1) * SUBLANES)
        qc = slice(c0, c0 + nc)
        fq = jnp.broadcast_to(fq_ref[0, h:h + 1, qi * tq + c0:qi * tq + c0 + nc], (SUBLANES, nc))
        m_old = m_ref[qi % 2, stat, qc]
        m_new = jnp.maximum(m_old, _sublane_allreduce(jnp.max(st3, axis=0), jnp.maximum) + fq)
        p = jnp.exp2(st3 - (m_new - fq)[None]).reshape(nk, nc).astype(BF16)
        alpha = jnp.exp2(m_old - m_new)
        m_ref[qi % 2, stat, qc] = m_new
        v_ones = jnp.concatenate([vt_ref[0, j, h * HEAD_DIM:(h + 1) * HEAD_DIM, :nk],
                                  ones_rows[:, :nk]], axis=0)
        pv = jnp.dot(v_ones, p, preferred_element_type=F32)[:ACC_ROWS]
        acc3 = acc_ref[qi % 2, h, :, qc].reshape(ACC_ROWS // SUBLANES, SUBLANES, nc)
        acc_ref[qi % 2, h, :, qc] = (acc3 * alpha[None]).reshape(ACC_ROWS, nc) + pv

    def end_block(qi):
        outs = []
        for h in range(HEADS):
            acc3 = acc_ref[qi % 2, h].reshape(ACC_ROWS // SUBLANES, SUBLANES, tq)
            outs.append((acc3[:HEAD_DIM // SUBLANES] / acc3[HEAD_DIM // SUBLANES][None])
                        .reshape(HEAD_DIM, tq))
        o_ref[qi * tq:(qi + 1) * tq, :] = jnp.concatenate(outs, axis=0).T.astype(BF16)

    items = []
    for qi in range(nq):
        for j in range(qi + 1):
            tiles = [(tk, 0, tq)] if j < qi else [(tk // 2, 0, tq // 2), (tk, tq // 2, tq // 2)]
            items += [(qi, j, h) + tile for h in range(HEADS) for tile in tiles]
    begun, pending = set(), []

    def issue(n):
        qi = items[n][0]
        if qi not in begun:
            begun.add(qi)
            begin_block(qi)
        pending.append(logits_t(*items[n]))

    for n in range(min(QK_LOOKAHEAD, len(items))):
        issue(n)
    for n, item in enumerate(items):
        if n + QK_LOOKAHEAD < len(items):
            issue(n + QK_LOOKAHEAD)
        softmax_pv(*item, pending.pop(0))
        if n + 1 == len(items) or items[n + 1][0] != item[0]:
            end_block(item[0])


def _attention(qt, k, ka, vt, fqt, batch, seq):
    tq, tk = TQ, TK
    assert tq == tk
    nq = seq // tq
    w = HEADS * HEAD_DIM
    return pl.pallas_call(
        functools.partial(_attn_kernel, tq=tq, tk=tk, nq=nq),
        grid=(batch,),
        in_specs=[pl.BlockSpec((1, nq, w, tq), lambda b: (b, 0, 0, 0)),
                  pl.BlockSpec((seq, w), lambda b: (b, 0)),
                  pl.BlockSpec((seq, LANES), lambda b: (b, 0)),
                  pl.BlockSpec((1, seq // tk, w, tk), lambda b: (b, 0, 0, 0)),
                  pl.BlockSpec((1, HEADS, seq), lambda b: (b, 0, 0))],
        out_specs=pl.BlockSpec((seq, w), lambda b: (b, 0)),
        out_shape=jax.ShapeDtypeStruct(k.shape, BF16),
        scratch_shapes=[pltpu.VMEM((2, HEADS, 2 * LANES, tq), BF16),
                        pltpu.VMEM((2, HEADS * SUBLANES, tq), F32),
                        pltpu.VMEM((2, HEADS, ACC_ROWS, tq), F32)],
        compiler_params=_params(("parallel",)),
        name="fox_attention",
    )(qt, k, ka, vt, fqt)


def _mix_kernel(a_ref, u_ref, up_ref, z_ref, zp_ref, cb_ref, gate_ref, x_ref,
                wpa_ref, pw_ref, ps_ref, wpp_ref, cw_ref, wpc_ref, wo_ref,
                o_ref, ue_ref, ze_ref, *, tm, tiles_per_seq):
    i = pl.program_id(0) % tiles_per_seq
    has_prev = i != 0
    pad, first = SUBLANES, SUBLANES + HALO

    u = u_ref[...].astype(F32)
    ue_ref[0:pad, :] = jnp.zeros((pad, BRANCH_W), F32)
    ue_ref[pad:first, :] = jnp.where(has_prev, up_ref[...].astype(F32), 0.0)
    ue_ref[first:, :] = u
    ze_ref[0:HALO, :] = jnp.where(has_prev, zp_ref[...].astype(F32), 0.0)
    ze_ref[HALO:, :] = z_ref[...].astype(F32)

    def branch_a(c):
        cs = slice(c * (D_MODEL // 4), (c + 1) * (D_MODEL // 4))
        return jnp.dot(a_ref[...], wpa_ref[:, cs], preferred_element_type=F32)

    y_a = [branch_a(0)]
    for level in range(len(POOL_WINDOWS) - 1):
        lag = 2 ** level
        cs = slice(level * POOL_GROUP_DIM, BRANCH_W)
        ue_ref[pad:, cs] = ue_ref[pad:, cs] + ue_ref[pad - lag:pad - lag + tm + HALO, cs]
        if level == 0:
            y_a.append(branch_a(1))
    y_a.append(branch_a(2))
    pos = i * tm + lax.broadcasted_iota(jnp.int32, (tm, 1), 0)
    ys = []
    for gi, w in enumerate(POOL_WINDOWS):
        cs = slice(gi * POOL_GROUP_DIM, (gi + 1) * POOL_GROUP_DIM)
        win = ue_ref[first:, cs]
        if gi == len(POOL_WINDOWS) - 1:
            win = win + ue_ref[first - w // 2:first - w // 2 + tm, cs]
        n_avail = jnp.minimum(pos + 1, w).astype(F32)
        d = win / n_avail - u[:, cs]
        y = jnp.dot(d.astype(BF16), pw_ref[gi], preferred_element_type=F32)
        ys.append((y * ps_ref[:, cs]).astype(BF16))
    y_a.append(branch_a(3))

    conv = cw_ref[CONV_K - 1:CONV_K, :] * ze_ref[HALO:, :]
    for lag in range(1, CONV_K):
        conv = conv + cw_ref[CONV_K - 1 - lag:CONV_K - lag, :] * ze_ref[HALO - lag:HALO - lag + tm, :]
    convd = (cb_ref[...].astype(F32) * conv).astype(BF16)
    y_b = jnp.dot(jnp.concatenate(ys, axis=1), wpp_ref[...], preferred_element_type=F32)
    y_c = jnp.dot(convd, wpc_ref[...], preferred_element_type=F32)
    y_a = jnp.concatenate(y_a, axis=1)

    mixed = None
    for n, y in enumerate((y_a, y_b, y_c)):
        gate2 = gate_ref[:, n * D_MODEL:(n + 1) * D_MODEL].astype(F32)
        mixed = gate2 * y if mixed is None else mixed + gate2 * y
    o_ref[...] = x_ref[...] + jnp.dot(mixed.astype(BF16), wo_ref[...], preferred_element_type=F32)


def _mix(a, u, z, cb, gate, x2, layer, wpa, pw, ps, wpp, cw, wpc, wo, seq):
    n = x2.shape[0]
    tm = TM_MIX
    tiles_per_seq = seq // tm
    row = lambda c: pl.BlockSpec((tm, c), lambda i: (i, 0))
    prev = pl.BlockSpec((HALO, BRANCH_W), lambda i: (jnp.maximum(i * (tm // HALO) - 1, 0), 0))
    return pl.pallas_call(
        functools.partial(_mix_kernel, tm=tm, tiles_per_seq=tiles_per_seq),
        grid=(n // tm,),
        in_specs=[row(BRANCH_W), row(BRANCH_W), prev, row(BRANCH_W), prev,
                  row(BRANCH_W), row(3 * D_MODEL), row(D_MODEL)]
                 + [_layer_spec(t, layer) for t in (wpa, pw, ps, wpp, cw, wpc, wo)],
        out_specs=row(D_MODEL),
        out_shape=jax.ShapeDtypeStruct(x2.shape, F32),
        scratch_shapes=[pltpu.VMEM((SUBLANES + HALO + tm, BRANCH_W), F32),
                        pltpu.VMEM((HALO + tm, BRANCH_W), F32)],
        compiler_params=_params(("parallel",)),
        name="branch_mix",
    )(a, u, u, z, z, cb, gate, x2, wpa, pw, ps, wpp, cw, wpc, wo)


def _ffn_kernel(x_ref, nw_ref, wgu_ref, wd_ref, fn_ref, o_ref, *, final):
    rows = x_ref.shape[0] // FFN_SUBTILES
    for s in range(FFN_SUBTILES):
        rs = slice(s * rows, (s + 1) * rows)
        x = x_ref[rs, :]
        h = _rms(x, nw_ref[...]).astype(BF16)
        acc = x
        for lo, hi in FFN_CHUNKS:
            a = jnp.dot(h, wgu_ref[:, lo:hi], preferred_element_type=F32)
            b = jnp.dot(h, wgu_ref[:, FFN_HIDDEN + lo:FFN_HIDDEN + hi], preferred_element_type=F32)
            act = (a * jax.nn.sigmoid(a) * b).astype(BF16)
            acc = acc + jnp.dot(act, wd_ref[lo:hi, :], preferred_element_type=F32)
        if final:
            acc = _rms(acc, fn_ref[...])
        o_ref[rs, :] = acc


def _ffn(x2, layer, nw, wgu, wd, fn, final):
    n = x2.shape[0]
    tm = TM_FFN
    row = pl.BlockSpec((tm, D_MODEL), lambda i: (i, 0))
    return pl.pallas_call(
        functools.partial(_ffn_kernel, final=final),
        grid=(n // tm,),
        in_specs=[row] + [_layer_spec(t, layer) for t in (nw, wgu, wd)] + [_const_spec(fn.shape)],
        out_specs=row,
        out_shape=jax.ShapeDtypeStruct(x2.shape, F32),
        compiler_params=_params(("parallel",)),
        name="swiglu_ffn",
    )(x2, nw, wgu, wd, fn)


def kernel(x, attn_norm, w_in, b_forget, b_gate, w_proj_attn, pool_w, pool_scale, w_proj_pool,
           conv_w, w_proj_conv, w_out, ffn_norm, w_gate_up, w_down, final_norm):
    batch, seq, d = x.shape
    depth = w_in.shape[0]
    x2 = x.reshape(batch * seq, d)
    w_main = _prep_w_in(w_in)
    b_f = jnp.pad(b_forget, ((0, 0), (0, LANES - HEADS))).reshape(depth, 1, LANES)
    b_g = (b_gate * 0.5).reshape(depth, 1, -1)
    w_pa, w_pp, w_pc = (w.astype(BF16) for w in (w_proj_attn, w_proj_pool, w_proj_conv))
    w_o = (w_out * 0.5).astype(BF16)
    w_pool = pool_w.astype(BF16)
    w_gu = w_gate_up.astype(BF16)
    w_d = w_down.astype(BF16)
    a_norm, f_norm = attn_norm.reshape(depth, 1, d), ffn_norm.reshape(depth, 1, d)
    p_scale = pool_scale.reshape(depth, 1, -1)
    fn = final_norm.reshape(1, d)
    for l in range(depth):
        qt, k, vt, u, z, cb, gate, lf = _inproj(x2, l, a_norm, w_main, b_f, b_g, batch, seq)
        fqt, ka = _cumsum(lf, batch, seq)
        a = _attention(qt, k, ka, vt, fqt, batch, seq)
        x2 = _mix(a, u, z, cb, gate, x2, l, w_pa, w_pool, p_scale, w_pp, conv_w, w_pc, w_o, seq)
        x2 = _ffn(x2, l, f_norm, w_gu, w_d, fn, final=(l == depth - 1))
    return x2.reshape(batch, seq, d)
```
